```python
import jax, jax.numpy as jnp
from jax import lax
import numpy as np

D_MODEL = 1024
BATCH = 2
SEQ = 8192
DEPTH = 1

N_META = 16
D_MIX = D_MODEL
GLA_WIDTH = D_MIX // 2
GLA_HEADS = 4
GLA_DV = GLA_WIDTH // GLA_HEADS
GLA_DK = GLA_DV // 2
GLA_KEY_WIDTH = GLA_HEADS * GLA_DK
GLA_GATE_RANK = 16
GLA_TAU = 16.0
GLA_CHUNK = 64
SWA_WIDTH = D_MIX - GLA_WIDTH
SWA_HEAD_DIM = 64
SWA_Q_HEADS = SWA_WIDTH // SWA_HEAD_DIM
SWA_KV_HEADS = 2
SWA_GROUP = SWA_Q_HEADS // SWA_KV_HEADS
SWA_KV_WIDTH = SWA_KV_HEADS * SWA_HEAD_DIM
WINDOW = 128
SWA_BLOCK = 128
ROPE_THETA = 10000.0
D_FF = 256 * ((8 * D_MODEL // 3 + 255) // 256)
NORM_EPS = 1e-6
NEG_INF = -1e30

IN_SPLITS = (GLA_KEY_WIDTH, GLA_KEY_WIDTH, GLA_WIDTH, GLA_WIDTH, GLA_GATE_RANK,
             SWA_WIDTH, SWA_KV_WIDTH, SWA_KV_WIDTH)
D_IN = sum(IN_SPLITS)

kernel_name = "hybrid_gla_swa_macaron_layer"


def rms_norm(x, w):
    xf = x.astype(jnp.float32)
    y = xf * lax.rsqrt(jnp.mean(xf * xf, axis=-1, keepdims=True) + NORM_EPS)
    return (y * w.astype(jnp.float32)).astype(x.dtype)


def swiglu(x, w_gate, w_up, w_down):
    return (jax.nn.silu(x @ w_gate) * (x @ w_up)) @ w_down


def rope(x, pos):
    hd = x.shape[-1]
    inv_freq = 1.0 / (ROPE_THETA ** (jnp.arange(0, hd, 2, dtype=jnp.float32) / hd))
    ang = pos.astype(jnp.float32)[:, None] * inv_freq[None, :]
    ang = jnp.concatenate([ang, ang], axis=-1)[:, None, :]
    xf = x.astype(jnp.float32)
    x1, x2 = jnp.split(xf, 2, axis=-1)
    rot = jnp.concatenate([-x2, x1], axis=-1)
    return (xf * jnp.cos(ang) + rot * jnp.sin(ang)).astype(x.dtype)


def gla_chunked(q, k, v, log_a):
    B, L, H, dk = q.shape
    dv = v.shape[-1]
    C = GLA_CHUNK
    pad = (-L) % C
    n = (L + pad) // C

    def to_chunks(t):
        t = jnp.pad(t, ((0, 0), (pad, 0), (0, 0), (0, 0)))
        return t.reshape(B, n, C, H, t.shape[-1]).transpose(1, 0, 3, 2, 4).astype(jnp.float32)

    qc = to_chunks(q) * (dk ** -0.5)
    kc = to_chunks(k)
    vc = to_chunks(v)
    bc = jnp.cumsum(to_chunks(log_a), axis=3)
    causal = jnp.tril(jnp.ones((C, C), dtype=bool))[None, None, :, :, None]

    def step(S, inp):
        qi, ki, vi, bi = inp
        diff = bi[:, :, :, None, :] - bi[:, :, None, :, :]
        decay = jnp.exp(jnp.where(causal, diff, -jnp.inf))
        attn = jnp.einsum('bhid,bhjd,bhijd->bhij', qi, ki, decay)
        o = jnp.einsum('bhij,bhjv->bhiv', attn, vi) + \
            jnp.einsum('bhid,bhdv->bhiv', qi * jnp.exp(bi), S)
        b_last = bi[:, :, -1:, :]
        S = jnp.exp(b_last[:, :, 0, :])[..., None] * S + \
            jnp.einsum('bhjd,bhjv->bhdv', ki * jnp.exp(b_last - bi), vi)
        return S, o

    S0 = jnp.zeros((B, H, dk, dv), jnp.float32)
    _, o = lax.scan(step, S0, (qc, kc, vc, bc))
    o = o.transpose(1, 0, 3, 2, 4).reshape(B, n * C, H, dv)[:, pad:]
    return o.astype(v.dtype)


def swa_with_sinks(q, k, v, sinks):
    B, L, HQ, hd = q.shape
    T = SWA_BLOCK
    KV, G = SWA_KV_HEADS, SWA_GROUP
    pad = (-L) % T
    Lp = L + pad
    nb = Lp // T
    qb = jnp.pad(q, ((0, 0), (pad, 0), (0, 0), (0, 0))).reshape(B, nb, T, KV, G, hd)
    kp = jnp.pad(k, ((0, 0), (pad + T, 0), (0, 0), (0, 0))).reshape(B, nb + 1, T, KV, hd)
    vp = jnp.pad(v, ((0, 0), (pad + T, 0), (0, 0), (0, 0))).reshape(B, nb + 1, T, KV, hd)
    k_band = jnp.concatenate([kp[:, :-1], kp[:, 1:]], axis=2)
    v_band = jnp.concatenate([vp[:, :-1], vp[:, 1:]], axis=2)
    k_meta = k[:, :N_META]
    v_meta = v[:, :N_META]

    qpos = (jnp.arange(Lp) - pad).reshape(nb, T)
    kpos_all = (jnp.arange(Lp + T) - pad - T).reshape(nb + 1, T)
    kpos = jnp.concatenate([kpos_all[:-1], kpos_all[1:]], axis=1)
    dq = qpos[:, :, None]
    dk_ = kpos[:, None, :]
    band_mask = (dk_ >= N_META) & (dk_ <= dq) & (dq - dk_ < WINDOW)
    meta_mask = jnp.arange(N_META)[None, None, :] <= dq
    mask = jnp.concatenate([band_mask, meta_mask], axis=-1)

    scale = hd ** -0.5
    s_band = jnp.einsum('bntkgd,bnskd->bnkgts', qb, k_band).astype(jnp.float32)
    s_meta = jnp.einsum('bntkgd,bmkd->bnkgtm', qb, k_meta).astype(jnp.float32)
    s = jnp.concatenate([s_band, s_meta], axis=-1) * scale
    s = jnp.where(mask[None, :, None, None], s, NEG_INF)
    sink = sinks.astype(jnp.float32).reshape(KV, G)[None, None, :, :, None, None]
    m = jnp.maximum(jnp.max(s, axis=-1, keepdims=True), sink)
    p = jnp.exp(s - m)
    p = p / (jnp.sum(p, axis=-1, keepdims=True) + jnp.exp(sink - m))
    p = p.astype(v.dtype)
    o = jnp.einsum('bnkgts,bnskd->bntkgd', p[..., :2 * T], v_band) + \
        jnp.einsum('bnkgtm,bmkd->bntkgd', p[..., 2 * T:], v_meta)
    return o.reshape(B, Lp, HQ, hd)[:, pad:]


def hybrid_mixer(h, w_in, gla_w_a2, gla_b_a, gla_out_norm, swa_sinks, swa_out_norm, w_out):
    B, L, _ = h.shape
    pos = jnp.arange(L, dtype=jnp.int32)
    proj = h @ w_in
    offsets = [int(o) for o in np.cumsum(IN_SPLITS)[:-1]]
    gq, gk, gv, gg, ga, sq, sk, sv = jnp.split(proj, offsets, axis=-1)

    log_a = jax.nn.log_sigmoid((ga @ gla_w_a2 + gla_b_a).astype(jnp.float32)) / GLA_TAU
    o_gla = gla_chunked(gq.reshape(B, L, GLA_HEADS, GLA_DK),
                        gk.reshape(B, L, GLA_HEADS, GLA_DK),
                        gv.reshape(B, L, GLA_HEADS, GLA_DV),
                        log_a.reshape(B, L, GLA_HEADS, GLA_DK))
    o_gla = rms_norm(o_gla, gla_out_norm) * jax.nn.silu(gg.reshape(B, L, GLA_HEADS, GLA_DV))
    o_gla = o_gla.reshape(B, L, GLA_WIDTH)

    q = rope(sq.reshape(B, L, SWA_Q_HEADS, SWA_HEAD_DIM), pos)
    k = rope(sk.reshape(B, L, SWA_KV_HEADS, SWA_HEAD_DIM), pos)
    v = sv.reshape(B, L, SWA_KV_HEADS, SWA_HEAD_DIM)
    o_swa = swa_with_sinks(q, k, v, swa_sinks).reshape(B, L, SWA_WIDTH)
    o_swa = rms_norm(o_swa, swa_out_norm)

    return jnp.concatenate([o_gla, o_swa], axis=-1) @ w_out


def setup_inputs(seed: int = 0) -> dict:
    key = jax.random.key(seed)
    ks = jax.random.split(key, 24)
    f32 = jnp.float32
    nrm = lambda k, shape, s: jax.random.normal(k, shape, f32) * s
    gain = lambda k, shape: 1.0 + 0.05 * jax.random.normal(k, shape, f32)
    Dp = DEPTH
    return {
        "x": jax.random.normal(ks[0], (BATCH, SEQ, D_MODEL), f32),
        "meta_tokens": nrm(ks[1], (N_META, D_MODEL), 1.0),
        "ffn1_pre_norm": gain(ks[2], (Dp, D_MODEL)),
        "ffn1_w_gate": nrm(ks[3], (Dp, D_MODEL, D_FF), D_MODEL ** -0.5),
        "ffn1_w_up": nrm(ks[4], (Dp, D_MODEL, D_FF), D_MODEL ** -0.5),
        "ffn1_w_down": nrm(ks[5], (Dp, D_FF, D_MODEL), D_FF ** -0.5),
        "ffn1_post_norm": gain(ks[6], (Dp, D_MODEL)),
        "mix_pre_norm": gain(ks[7], (Dp, D_MODEL)),
        "w_in": nrm(ks[8], (Dp, D_MODEL, D_IN), D_MODEL ** -0.5),
        "gla_w_a2": nrm(ks[9], (Dp, GLA_GATE_RANK, GLA_KEY_WIDTH), GLA_GATE_RANK ** -0.5),
        "gla_b_a": nrm(ks[10], (Dp, GLA_KEY_WIDTH), 0.1),
        "gla_out_norm": gain(ks[11], (Dp, GLA_DV)),
        "swa_sinks": nrm(ks[12], (Dp, SWA_Q_HEADS), 0.5),
        "swa_out_norm": gain(ks[13], (Dp, SWA_WIDTH)),
        "w_out": nrm(ks[14], (Dp, D_MIX, D_MODEL), D_MIX ** -0.5),
        "mix_post_norm": gain(ks[15], (Dp, D_MODEL)),
        "ffn2_pre_norm": gain(ks[16], (Dp, D_MODEL)),
        "ffn2_w_gate": nrm(ks[17], (Dp, D_MODEL, D_FF), D_MODEL ** -0.5),
        "ffn2_w_up": nrm(ks[18], (Dp, D_MODEL, D_FF), D_MODEL ** -0.5),
        "ffn2_w_down": nrm(ks[19], (Dp, D_FF, D_MODEL), D_FF ** -0.5),
        "ffn2_post_norm": gain(ks[20], (Dp, D_MODEL)),
    }


def reference(x, meta_tokens, ffn1_pre_norm, ffn1_w_gate, ffn1_w_up, ffn1_w_down, ffn1_post_norm,
              mix_pre_norm, w_in, gla_w_a2, gla_b_a, gla_out_norm, swa_sinks, swa_out_norm, w_out,
              mix_post_norm, ffn2_pre_norm, ffn2_w_gate, ffn2_w_up, ffn2_w_down, ffn2_post_norm):
    B = x.shape[0]
    meta = jnp.broadcast_to(meta_tokens.astype(x.dtype)[None], (B, N_META, x.shape[-1]))
    h = jnp.concatenate([meta, x], axis=1)
    for l in range(DEPTH):
        f = swiglu(rms_norm(h, ffn1_pre_norm[l]), ffn1_w_gate[l], ffn1_w_up[l], ffn1_w_down[l])
        h = h + 0.5 * rms_norm(f, ffn1_post_norm[l])
        m = hybrid_mixer(rms_norm(h, mix_pre_norm[l]), w_in[l], gla_w_a2[l], gla_b_a[l],
                         gla_out_norm[l], swa_sinks[l], swa_out_norm[l], w_out[l])
        h = h + rms_norm(m, mix_post_norm[l])
        f = swiglu(rms_norm(h, ffn2_pre_norm[l]), ffn2_w_gate[l], ffn2_w_up[l], ffn2_w_down[l])
        h = h + 0.5 * rms_norm(f, ffn2_post_norm[l])
    return h[:, N_META:]
```

```python
import functools

import numpy as np
import jax
import jax.numpy as jnp
from jax import lax
from jax.experimental import pallas as pl
from jax.experimental.pallas import tpu as pltpu

F32 = jnp.float32
BF16 = jnp.bfloat16

D_MODEL = 1024
D_FF = 2816
N_META = 16
NORM_EPS = 1e-6
NEG_INF = -1e30
ROPE_THETA = 10000.0

GLA_HEADS = 4
GLA_DK = 64
GLA_DV = 128
GLA_KW = GLA_HEADS * GLA_DK
GLA_VW = GLA_HEADS * GLA_DV
GLA_TAU = 16.0
GLA_CHUNK = 64
GLA_SUB = 8
GLA_ROWS = 512

SWA_HEAD_DIM = 64
SWA_Q_HEADS = 8
SWA_KV_HEADS = 2
SWA_QW = SWA_Q_HEADS * SWA_HEAD_DIM
SWA_KW = SWA_KV_HEADS * SWA_HEAD_DIM
SWA_BLOCK = 128

LANES = 128
GA_PAD = LANES
FF_CHUNK = 256

VMEM_LIMIT = 56 * 1024 * 1024


def _dot(a, b):
    return jnp.dot(a, b, preferred_element_type=F32)


def _dot_nt(a, b):
    return lax.dot_general(a, b, (((1,), (1,)), ((), ())), preferred_element_type=F32)


def _dot_tn(a, b):
    return lax.dot_general(a, b, (((0,), (0,)), ((), ())), preferred_element_type=F32)


def _rms(x, w):
    ms = jnp.mean(x * x, axis=-1, keepdims=True)
    return x * lax.rsqrt(ms + NORM_EPS) * w


def _const_spec(shape):
    return pl.BlockSpec(shape, lambda *_: (0,) * len(shape), pipeline_mode=pl.Buffered(1))


def _ffn_body(x_ref, pre_ref, wg_ref, wu_ref, wd_ref, post_ref, o_ref, act_ref):
    x = x_ref[...]
    xn = _rms(x, pre_ref[...]).astype(BF16)
    for c in range(D_FF // FF_CHUNK):
        cols = slice(c * FF_CHUNK, (c + 1) * FF_CHUNK)
        g = _dot(xn, wg_ref[:, cols])
        u = _dot(xn, wu_ref[:, cols])
        act_ref[:, cols] = (g * jax.nn.sigmoid(g) * u).astype(BF16)
    f = _dot(act_ref[...], wd_ref[...])
    o_ref[...] = x + 0.5 * _rms(f, post_ref[...])


def _ffn_call(x, pre, wg, wu, wd, post, *, tm):
    rows = x.shape[0]
    return pl.pallas_call(
        _ffn_body,
        grid=(rows // tm,),
        in_specs=[
            pl.BlockSpec((tm, D_MODEL), lambda i: (i, 0)),
            _const_spec((1, D_MODEL)),
            _const_spec((D_MODEL, D_FF)),
            _const_spec((D_MODEL, D_FF)),
            _const_spec((D_FF, D_MODEL)),
            _const_spec((1, D_MODEL)),
        ],
        out_specs=pl.BlockSpec((tm, D_MODEL), lambda i: (i, 0)),
        out_shape=jax.ShapeDtypeStruct((rows, D_MODEL), F32),
        scratch_shapes=[pltpu.VMEM((tm, D_FF), BF16)],
        compiler_params=pltpu.CompilerParams(
            dimension_semantics=("arbitrary",), vmem_limit_bytes=VMEM_LIMIT),
        name=f"ffn_tm{tm}",
    )(x, pre, wg, wu, wd, post)


def _rope(x, cos, sin_signed, first_half):
    out = []
    for g in range(x.shape[1] // LANES):
        xg = x[:, g * LANES:(g + 1) * LANES]
        fwd = pltpu.roll(xg, LANES - 32, axis=1)
        bwd = pltpu.roll(xg, 32, axis=1)
        out.append(xg * cos + jnp.where(first_half, fwd, bwd) * sin_signed)
    return out[0] if len(out) == 1 else jnp.concatenate(out, axis=1)


def _proj_body(h_ref, nw_ref, wgla_ref, wga_ref, wa2_ref, ba_ref, wswa_ref, cos_ref, sin_ref,
               gq_ref, gk_ref, gv_ref, gg_ref, la_ref, sq_ref, sk_ref, sv_ref):
    hn = _rms(h_ref[...], nw_ref[...]).astype(BF16)
    pg = _dot(hn, wgla_ref[...])
    gq_ref[...] = (pg[:, 0:GLA_KW] * (GLA_DK ** -0.5)).astype(BF16)
    gk_ref[...] = pg[:, GLA_KW:2 * GLA_KW].astype(BF16)
    gv_ref[...] = pg[:, 2 * GLA_KW:2 * GLA_KW + GLA_VW].astype(BF16)
    gg_ref[...] = pg[:, 2 * GLA_KW + GLA_VW:].astype(BF16)

    ga = _dot(hn, wga_ref[...]).astype(BF16)
    z = _dot(ga, wa2_ref[...]) + ba_ref[...]
    log_sig = jnp.minimum(z, 0.0) - jnp.log1p(jnp.exp(-jnp.abs(z)))
    la_ref[...] = log_sig * (1.0 / GLA_TAU)

    ps = _dot(hn, wswa_ref[...])
    cos = cos_ref[...]
    sin_signed = sin_ref[...]
    lane = lax.broadcasted_iota(jnp.int32, (1, LANES), 1)
    first_half = (lane % SWA_HEAD_DIM) < (SWA_HEAD_DIM // 2)
    sq = _rope(ps[:, 0:SWA_QW], cos, sin_signed, first_half)
    sq_ref[...] = (sq * (SWA_HEAD_DIM ** -0.5)).astype(BF16)
    sk_ref[...] = _rope(ps[:, SWA_QW:SWA_QW + SWA_KW], cos, sin_signed, first_half).astype(BF16)
    sv_ref[...] = ps[:, SWA_QW + SWA_KW:].astype(BF16)


def _proj_call(h, nw, wgla, wga, wa2, ba, wswa, cos, sin_signed, *, tm, seq_rows):
    rows = h.shape[0]
    seq_blocks = seq_rows // tm
    row_spec = lambda w: pl.BlockSpec((tm, w), lambda i: (i, 0))
    tab_spec = pl.BlockSpec((tm, LANES), lambda i: (i % seq_blocks, 0))
    widths = (GLA_KW, GLA_KW, GLA_VW, GLA_VW, GLA_KW, SWA_QW, SWA_KW, SWA_KW)
    dtypes = (BF16, BF16, BF16, BF16, F32, BF16, BF16, BF16)
    return pl.pallas_call(
        _proj_body,
        grid=(rows // tm,),
        in_specs=[
            row_spec(D_MODEL),
            _const_spec((1, D_MODEL)),
            _const_spec(wgla.shape),
            _const_spec(wga.shape),
            _const_spec(wa2.shape),
            _const_spec((1, GLA_KW)),
            _const_spec(wswa.shape),
            tab_spec,
            tab_spec,
        ],
        out_specs=[row_spec(w) for w in widths],
        out_shape=[jax.ShapeDtypeStruct((rows, w), dt) for w, dt in zip(widths, dtypes)],
        compiler_params=pltpu.CompilerParams(
            dimension_semantics=("arbitrary",), vmem_limit_bytes=VMEM_LIMIT),
        name=f"proj_tm{tm}",
    )(h, nw, wgla, wga, wa2, ba, wswa, cos, sin_signed)


_FAR_KEY_SLABS = ((0, 32, 31), (0, 8, 7), (0, 16, 15), (0, 24, 23), (32, 40, 39), (32, 48, 47), (32, 56, 55))
_FAR_QUERY_SLABS = ((32, 64), (8, 32), (40, 64))
_FAR_QUERY_REFS = ((31, 32), (7, 8), (15, 8), (23, 8), (39, 8), (47, 8), (55, 8))
_FAR_Q = sum(b - a for a, b in _FAR_QUERY_SLABS)
_FAR_K = sum(b - a for a, b, _ in _FAR_KEY_SLABS)


def _far_mask_np():
    m = np.zeros((_FAR_Q, _FAR_K), np.float32)
    m[0:32, 0:32] = 1.0
    col = 32
    row = 32
    for _half in range(2):
        for grp in range(1, 4):
            m[row:row + 8, col:col + 8 * grp] = 1.0
            row += 8
            col += 8 * grp
    return m


def _tri_blockdiag_np(rows):
    r = np.arange(rows)
    same = (r[:, None] // GLA_CHUNK) == (r[None, :] // GLA_CHUNK)
    return (same & (r[None, :] <= r[:, None])).astype(np.float32)


def _near_select_np():
    s = np.zeros((GLA_SUB, GLA_HEADS, GLA_DK, GLA_HEADS, GLA_CHUNK // GLA_SUB, GLA_SUB), np.float32)
    for jl in range(GLA_SUB):
        for h in range(GLA_HEADS):
            s[jl, h, :, h, :, jl] = 1.0
    return s.reshape(GLA_SUB * GLA_KW, GLA_KW)


def _cumsum_rows(tri, la):
    hi = la.astype(BF16)
    lo = (la - hi.astype(F32)).astype(BF16)
    return _dot(tri, hi) + _dot(tri, lo)


def _state_update(state_t, kc, vc_bf, bc, head_lane):
    b_last = bc[GLA_CHUNK - 1:GLA_CHUNK, :]
    k2 = (kc * jnp.exp(b_last - bc)).astype(BF16)
    u = _dot_tn(vc_bf, k2)
    upd = jnp.zeros((GLA_DV, GLA_KW), F32)
    for h in range(GLA_HEADS):
        upd = upd + jnp.where(head_lane[h], u[h * GLA_DV:(h + 1) * GLA_DV, :], 0.0)
    return state_t * jnp.exp(b_last) + upd


def _gla_chunk(qc, kc, vc, bc, near, state_t, head_lane, far_mask, vbd_mask):
    vc_bf = vc.astype(BF16)

    def ref_rows(r, n):
        return jnp.broadcast_to(bc[r:r + 1, :], (n, GLA_KW))

    qb = (qc * jnp.exp(bc)).astype(BF16)
    lhs = jnp.concatenate([jnp.where(head_lane[h], qb, 0) for h in range(GLA_HEADS)], axis=0)
    o_inter = _dot_nt(lhs, state_t.astype(BF16))

    q_rows = jnp.concatenate([qc[a:b] for a, b in _FAR_QUERY_SLABS], axis=0)
    bq = jnp.concatenate([bc[a:b] for a, b in _FAR_QUERY_SLABS], axis=0)
    rq = jnp.concatenate([ref_rows(r, n) for r, n in _FAR_QUERY_REFS], axis=0)
    q_far = (q_rows * jnp.exp(bq - rq)).astype(BF16)
    k_rows = jnp.concatenate([kc[a:b] for a, b, _ in _FAR_KEY_SLABS], axis=0)
    bk = jnp.concatenate([bc[a:b] for a, b, _ in _FAR_KEY_SLABS], axis=0)
    rk = jnp.concatenate([ref_rows(r, b - a) for a, b, r in _FAR_KEY_SLABS], axis=0)
    k_far = k_rows * jnp.exp(rk - bk)
    v_far = jnp.concatenate([vc[a:b] for a, b, _ in _FAR_KEY_SLABS], axis=0).astype(BF16)

    v_bd = jnp.where(vbd_mask, jnp.concatenate([vc_bf] * GLA_HEADS, axis=0), 0)
    o_near = _dot(near, v_bd)

    zeros8 = jnp.zeros((8, GLA_DV), F32)
    outs = []
    for h in range(GLA_HEADS):
        kh = jnp.where(head_lane[h], k_far, 0.0).astype(BF16)
        p = _dot_nt(q_far, kh)
        p = jnp.where(far_mask, p, 0.0).astype(BF16)
        of = _dot(p, v_far[:, h * GLA_DV:(h + 1) * GLA_DV])
        top = jnp.concatenate([zeros8, of[32:56]], axis=0)
        bot = of[0:32] + jnp.concatenate([zeros8, of[56:80]], axis=0)
        outs.append(jnp.concatenate([top, bot], axis=0) + o_inter[h * GLA_CHUNK:(h + 1) * GLA_CHUNK])
    o = jnp.concatenate(outs, axis=1) + o_near
    return o, _state_update(state_t, kc, vc_bf, bc, head_lane)


def _gla_body(q_ref, k_ref, v_ref, la_ref, km_ref, vm_ref, lam_ref, tri_ref, sel_ref, fmask_ref,
              o_ref, state_ref):
    lane = lax.broadcasted_iota(jnp.int32, (1, GLA_KW), 1)
    head_lane = [(lane // GLA_DK) == h for h in range(GLA_HEADS)]

    @pl.when(pl.program_id(1) == 0)
    def _():
        tri64 = tri_ref[0:GLA_CHUNK, 0:GLA_CHUNK]
        bm = _cumsum_rows(tri64, lam_ref[...])
        zero = jnp.zeros((GLA_DV, GLA_KW), F32)
        state_ref[...] = _state_update(zero, km_ref[...].astype(F32), vm_ref[...], bm, head_lane)

    q = q_ref[...].astype(F32)
    k = k_ref[...].astype(F32)
    v = v_ref[...].astype(F32)
    b = _cumsum_rows(tri_ref[...], la_ref[...])

    nblk = GLA_ROWS // GLA_SUB
    q3 = q.reshape(nblk, GLA_SUB, GLA_KW)
    k3 = k.reshape(nblk, GLA_SUB, GLA_KW)
    b3 = b.reshape(nblk, GLA_SUB, GLA_KW)
    sub = lax.broadcasted_iota(jnp.int32, (nblk, GLA_SUB, GLA_KW), 1)
    terms = []
    for jl in range(GLA_SUB):
        decay = jnp.exp(jnp.minimum(b3 - b3[:, jl:jl + 1, :], 0.0))
        t = jnp.where(sub >= jl, q3 * k3[:, jl:jl + 1, :] * decay, 0.0)
        terms.append(t.reshape(GLA_ROWS, GLA_KW).astype(BF16))
    near = _dot(jnp.concatenate(terms, axis=1), sel_ref[...])
    row = lax.broadcasted_iota(jnp.int32, (GLA_ROWS, GLA_KW), 0)
    col = lax.broadcasted_iota(jnp.int32, (GLA_ROWS, GLA_KW), 1)
    same_block = ((row // GLA_SUB) % (GLA_CHUNK // GLA_SUB)) == ((col // GLA_SUB) % (GLA_CHUNK // GLA_SUB))
    near = jnp.where(same_block, near, 0.0).astype(BF16)

    far_mask = fmask_ref[...] > 0.5
    vrow = lax.broadcasted_iota(jnp.int32, (GLA_HEADS * GLA_CHUNK, GLA_VW), 0)
    vcol = lax.broadcasted_iota(jnp.int32, (GLA_HEADS * GLA_CHUNK, GLA_VW), 1)
    vbd_mask = (vrow // GLA_CHUNK) == (vcol // GLA_DV)

    state_t = state_ref[...]
    for c in range(GLA_ROWS // GLA_CHUNK):
        rows = slice(c * GLA_CHUNK, (c + 1) * GLA_CHUNK)
        o, state_t = _gla_chunk(q[rows], k[rows], v[rows], b[rows], near[rows], state_t,
                                head_lane, far_mask, vbd_mask)
        o_ref[rows, :] = o.astype(BF16)
    state_ref[...] = state_t


def _gla_call(gq, gk, gv, la, km, vm, lam, *, batch, seq_rows):
    nsteps = seq_rows // GLA_ROWS
    row_spec = lambda w: pl.BlockSpec((GLA_ROWS, w), lambda b, i: (b * nsteps + i, 0))
    tri = jnp.asarray(_tri_blockdiag_np(GLA_ROWS), BF16)
    sel = jnp.asarray(_near_select_np(), BF16)
    fmask = jnp.asarray(_far_mask_np(), F32)
    return pl.pallas_call(
        _gla_body,
        grid=(batch, nsteps),
        in_specs=[
            row_spec(GLA_KW), row_spec(GLA_KW), row_spec(GLA_VW), row_spec(GLA_KW),
            _const_spec((GLA_CHUNK, GLA_KW)), _const_spec((GLA_CHUNK, GLA_VW)), _const_spec((GLA_CHUNK, GLA_KW)),
            _const_spec(tri.shape), _const_spec(sel.shape), _const_spec(fmask.shape),
        ],
        out_specs=row_spec(GLA_VW),
        out_shape=jax.ShapeDtypeStruct((batch * seq_rows, GLA_VW), BF16),
        scratch_shapes=[pltpu.VMEM((GLA_DV, GLA_KW), F32)],
        compiler_params=pltpu.CompilerParams(
            dimension_semantics=("arbitrary", "arbitrary"), vmem_limit_bytes=VMEM_LIMIT),
        name="gla",
    )(gq, gk, gv, la, km, vm, lam, tri, sel, fmask)


def _half_variants(x, low_lane):
    swapped = pltpu.roll(x, SWA_HEAD_DIM, axis=1)
    z = jnp.zeros_like(x)
    return (
        (jnp.where(low_lane, x, z).astype(BF16), jnp.where(low_lane, z, swapped).astype(BF16)),
        (jnp.where(low_lane, swapped, z).astype(BF16), jnp.where(low_lane, z, x).astype(BF16)),
    )


def _swa_body(sink_ref, q_ref, kp_ref, kc_ref, vp_ref, vc_ref, km_ref, vm_ref, o_ref):
    has_prev = pl.program_id(1) > 0
    t = SWA_BLOCK
    lane = lax.broadcasted_iota(jnp.int32, (1, LANES), 1)
    low_lane = lane < SWA_HEAD_DIM
    k_band = jnp.concatenate([kp_ref[...], kc_ref[...]], axis=0).astype(F32)
    v_band = jnp.concatenate([vp_ref[...], vc_ref[...]], axis=0).astype(F32)
    k_var = _half_variants(k_band, low_lane)
    v_var = _half_variants(v_band, low_lane)
    km_var = _half_variants(km_ref[...].astype(F32), low_lane)
    vm_var = _half_variants(vm_ref[...].astype(F32), low_lane)

    r = lax.broadcasted_iota(jnp.int32, (t, t), 0)
    c = lax.broadcasted_iota(jnp.int32, (t, t), 1)
    causal = c <= r
    use_prev = jnp.logical_and(jnp.logical_not(causal), has_prev)

    q = q_ref[...]
    for g in range(SWA_KV_HEADS):
        qg = q[:, g * 2 * LANES:(g + 1) * 2 * LANES]
        lhs = jnp.concatenate([qg[:, 0:LANES], qg[:, LANES:2 * LANES]], axis=0)
        groups = [jnp.zeros((t, LANES), F32), jnp.zeros((t, LANES), F32)]
        for half in range(2):
            s_all = _dot_nt(lhs, k_var[g][half])
            sm_all = _dot_nt(lhs, km_var[g][half])
            for grp in range(2):
                head = 4 * g + 2 * grp + half
                sink = sink_ref[0, head]
                s = s_all[grp * t:(grp + 1) * t]
                sm = sm_all[grp * t:(grp + 1) * t]
                s_sel = jnp.where(causal, s[:, t:2 * t], jnp.where(use_prev, s[:, 0:t], NEG_INF))
                m = jnp.maximum(jnp.max(s_sel, axis=-1, keepdims=True), jnp.max(sm, axis=-1, keepdims=True))
                m = jnp.maximum(m, sink)
                p = jnp.exp(s_sel - m)
                pm = jnp.exp(sm - m)
                denom = jnp.sum(p, axis=-1, keepdims=True) + jnp.sum(pm, axis=-1, keepdims=True) + jnp.exp(sink - m)
                p_band = jnp.concatenate([jnp.where(causal, 0.0, p), jnp.where(causal, p, 0.0)], axis=1)
                o = _dot(p_band.astype(BF16), v_var[g][half]) + _dot(pm.astype(BF16), vm_var[g][half])
                groups[grp] = groups[grp] + o / denom
        for grp in range(2):
            lo = (2 * g + grp) * LANES
            o_ref[:, lo:lo + LANES] = groups[grp].astype(BF16)


def _swa_call(sq, sk, sv, km, vm, sinks, *, batch, seq_rows):
    nb = seq_rows // SWA_BLOCK
    cur = lambda w: pl.BlockSpec((SWA_BLOCK, w), lambda b, i: (b * nb + i, 0))
    prev = lambda w: pl.BlockSpec((SWA_BLOCK, w), lambda b, i: (b * nb + jnp.maximum(i - 1, 0), 0))
    return pl.pallas_call(
        _swa_body,
        grid=(batch, nb),
        in_specs=[
            pl.BlockSpec(memory_space=pltpu.SMEM),
            cur(SWA_QW), prev(SWA_KW), cur(SWA_KW), prev(SWA_KW), cur(SWA_KW),
            _const_spec((N_META, SWA_KW)), _const_spec((N_META, SWA_KW)),
        ],
        out_specs=cur(SWA_QW),
        out_shape=jax.ShapeDtypeStruct((batch * seq_rows, SWA_QW), BF16),
        compiler_params=pltpu.CompilerParams(
            dimension_semantics=("arbitrary", "arbitrary"), vmem_limit_bytes=VMEM_LIMIT),
        name="swa",
    )(sinks, sq, sk, sk, sv, sv, km, vm)


def _out_body(og_ref, gg_ref, os_ref, h_ref, gnw_ref, snw_ref, wout_ref, pnw_ref, o_ref):
    og = og_ref[...].astype(F32)
    gg = gg_ref[...].astype(F32)
    gnw = gnw_ref[...]
    parts = []
    for h in range(GLA_HEADS):
        cols = slice(h * GLA_DV, (h + 1) * GLA_DV)
        gate = gg[:, cols]
        parts.append(_rms(og[:, cols], gnw) * (gate * jax.nn.sigmoid(gate)))
    parts.append(_rms(os_ref[...].astype(F32), snw_ref[...]))
    mixed = jnp.concatenate(parts, axis=1).astype(BF16)
    m = _dot(mixed, wout_ref[...])
    o_ref[...] = h_ref[...] + _rms(m, pnw_ref[...])


def _out_call(og, gg, osw, h, gnw, snw, wout, pnw, *, tm):
    rows = h.shape[0]
    row_spec = lambda w: pl.BlockSpec((tm, w), lambda i: (i, 0))
    return pl.pallas_call(
        _out_body,
        grid=(rows // tm,),
        in_specs=[
            row_spec(GLA_VW), row_spec(GLA_VW), row_spec(SWA_QW), row_spec(D_MODEL),
            _const_spec((1, GLA_DV)), _const_spec((1, SWA_QW)), _const_spec((D_MODEL, D_MODEL)),
            _const_spec((1, D_MODEL)),
        ],
        out_specs=row_spec(D_MODEL),
        out_shape=jax.ShapeDtypeStruct((rows, D_MODEL), F32),
        compiler_params=pltpu.CompilerParams(
            dimension_semantics=("arbitrary",), vmem_limit_bytes=VMEM_LIMIT),
        name="mix_out",
    )(og, gg, osw, h, gnw, snw, wout, pnw)


def _rope_tables(positions):
    inv_freq = 1.0 / (ROPE_THETA ** (jnp.arange(0, SWA_HEAD_DIM, 2, dtype=F32) / SWA_HEAD_DIM))
    ang = positions.astype(F32)[:, None] * inv_freq[None, :]
    ang = jnp.concatenate([ang, ang, ang, ang], axis=-1)
    sign = jnp.where((jnp.arange(LANES) % SWA_HEAD_DIM) < SWA_HEAD_DIM // 2, -1.0, 1.0).astype(F32)
    return jnp.cos(ang), jnp.sin(ang) * sign[None, :]


def kernel(x, meta_tokens, ffn1_pre_norm, ffn1_w_gate, ffn1_w_up, ffn1_w_down, ffn1_post_norm, mix_pre_norm, w_in, gla_w_a2, gla_b_a, gla_out_norm, swa_sinks, swa_out_norm, w_out, mix_post_norm, ffn2_pre_norm, ffn2_w_gate, ffn2_w_up, ffn2_w_down, ffn2_post_norm):
    batch, seq, _ = x.shape
    depth = ffn1_pre_norm.shape[0]
    assert depth == 1 and seq % GLA_ROWS == 0
    row = lambda a: a.reshape(1, -1).astype(F32)
    tm = 512

    w = w_in[0]
    n_gla = 2 * GLA_KW + 2 * GLA_VW
    wgla = w[:, :n_gla].astype(BF16)
    wga = jnp.pad(w[:, n_gla:n_gla + N_META], ((0, 0), (0, GA_PAD - N_META))).astype(BF16)
    wswa = w[:, n_gla + N_META:].astype(BF16)
    wa2 = jnp.pad(gla_w_a2[0], ((0, GA_PAD - N_META), (0, 0))).astype(BF16)
    ba = row(gla_b_a[0])

    ffn1 = (row(ffn1_pre_norm[0]), ffn1_w_gate[0].astype(BF16), ffn1_w_up[0].astype(BF16),
            ffn1_w_down[0].astype(BF16), row(ffn1_post_norm[0]))
    ffn2 = (row(ffn2_pre_norm[0]), ffn2_w_gate[0].astype(BF16), ffn2_w_up[0].astype(BF16),
            ffn2_w_down[0].astype(BF16), row(ffn2_post_norm[0]))
    proj_w = (row(mix_pre_norm[0]), wgla, wga, wa2, ba, wswa)

    hm = _ffn_call(meta_tokens.astype(F32), *ffn1, tm=N_META)
    cos_m, sin_m = _rope_tables(jnp.arange(N_META))
    _, gk_m, gv_m, _, la_m, _, sk_m, sv_m = _proj_call(hm, *proj_w, cos_m, sin_m, tm=N_META, seq_rows=N_META)
    front = GLA_CHUNK - N_META
    pad_front = lambda a: jnp.pad(a, ((front, 0), (0, 0)))

    xs = x.reshape(batch * seq, D_MODEL)
    h1 = _ffn_call(xs, *ffn1, tm=tm)
    cos_s, sin_s = _rope_tables(N_META + jnp.arange(seq))
    gq, gk, gv, gg, la, sq, sk, sv = _proj_call(h1, *proj_w, cos_s, sin_s, tm=tm, seq_rows=seq)
    o_gla = _gla_call(gq, gk, gv, la, pad_front(gk_m), pad_front(gv_m), pad_front(la_m),
                      batch=batch, seq_rows=seq)
    o_swa = _swa_call(sq, sk, sv, sk_m, sv_m, swa_sinks.reshape(1, SWA_Q_HEADS).astype(F32),
                      batch=batch, seq_rows=seq)
    h2 = _out_call(o_gla, gg, o_swa, h1, row(gla_out_norm[0]), row(swa_out_norm[0]),
                   w_out[0].astype(BF16), row(mix_post_norm[0]), tm=tm)
    out = _ffn_call(h2, *ffn2, tm=tm)
    return out.reshape(batch, seq, D_MODEL)
```

```python
import functools

import numpy as np
import jax
import jax.numpy as jnp
from jax import lax
from jax.experimental import pallas as pl
from jax.experimental.pallas import tpu as pltpu

F32 = jnp.float32
BF16 = jnp.bfloat16

D_MODEL = 1024
D_FF = 2816
N_META = 16
NORM_EPS = 1e-6
NEG_INF = -1e30
ROPE_THETA = 10000.0

GLA_HEADS = 4
GLA_DK = 64
GLA_DV = 128
GLA_KW = GLA_HEADS * GLA_DK
GLA_VW = GLA_HEADS * GLA_DV
GLA_TAU = 16.0
GLA_CHUNK = 64
GLA_SUB = 8
GLA_ROWS = 512

SWA_HEAD_DIM = 64
SWA_Q_HEADS = 8
SWA_KV_HEADS = 2
SWA_QW = SWA_Q_HEADS * SWA_HEAD_DIM
SWA_KW = SWA_KV_HEADS * SWA_HEAD_DIM
SWA_BLOCK = 128
SWA_QBLOCKS = 4

LANES = 128
GA_PAD = LANES
FF_CHUNK = 256

VMEM_LIMIT = 56 * 1024 * 1024


def _dot(a, b):
    return jnp.dot(a, b, preferred_element_type=F32)


def _dot_nt(a, b):
    return lax.dot_general(a, b, (((1,), (1,)), ((), ())), preferred_element_type=F32)


def _dot_tn(a, b):
    return lax.dot_general(a, b, (((0,), (0,)), ((), ())), preferred_element_type=F32)


def _rms(x, w):
    ms = jnp.mean(x * x, axis=-1, keepdims=True)
    return x * lax.rsqrt(ms + NORM_EPS) * w


def _const_spec(shape):
    return pl.BlockSpec(shape, lambda *_: (0,) * len(shape), pipeline_mode=pl.Buffered(1))


def _ffn_body(x_ref, pre_ref, wg_ref, wu_ref, wd_ref, post_ref, o_ref, act_ref):
    x = x_ref[...]
    xn = _rms(x, pre_ref[...]).astype(BF16)
    for c in range(D_FF // FF_CHUNK):
        cols = slice(c * FF_CHUNK, (c + 1) * FF_CHUNK)
        g = _dot(xn, wg_ref[:, cols])
        u = _dot(xn, wu_ref[:, cols])
        act_ref[:, cols] = (g * jax.nn.sigmoid(g) * u).astype(BF16)
    f = _dot(act_ref[...], wd_ref[...])
    o_ref[...] = x + 0.5 * _rms(f, post_ref[...])


def _ffn_call(x, pre, wg, wu, wd, post, *, tm):
    rows = x.shape[0]
    return pl.pallas_call(
        _ffn_body,
        grid=(rows // tm,),
        in_specs=[
            pl.BlockSpec((tm, D_MODEL), lambda i: (i, 0)),
            _const_spec((1, D_MODEL)),
            _const_spec((D_MODEL, D_FF)),
            _const_spec((D_MODEL, D_FF)),
            _const_spec((D_FF, D_MODEL)),
            _const_spec((1, D_MODEL)),
        ],
        out_specs=pl.BlockSpec((tm, D_MODEL), lambda i: (i, 0)),
        out_shape=jax.ShapeDtypeStruct((rows, D_MODEL), F32),
        scratch_shapes=[pltpu.VMEM((tm, D_FF), BF16)],
        compiler_params=pltpu.CompilerParams(
            dimension_semantics=("arbitrary",), vmem_limit_bytes=VMEM_LIMIT),
        name=f"ffn_tm{tm}",
    )(x, pre, wg, wu, wd, post)


def _rope(x, cos, sin_signed, first_half):
    out = []
    for g in range(x.shape[1] // LANES):
        xg = x[:, g * LANES:(g + 1) * LANES]
        fwd = pltpu.roll(xg, LANES - 32, axis=1)
        bwd = pltpu.roll(xg, 32, axis=1)
        out.append(xg * cos + jnp.where(first_half, fwd, bwd) * sin_signed)
    return out[0] if len(out) == 1 else jnp.concatenate(out, axis=1)


def _proj_body(h_ref, nw_ref, wgla_ref, wga_ref, wa2_ref, ba_ref, wswa_ref, cos_ref, sin_ref,
               gq_ref, gk_ref, gv_ref, gg_ref, la_ref, sq_ref, sk_ref, sv_ref):
    hn = _rms(h_ref[...], nw_ref[...]).astype(BF16)
    pg = _dot(hn, wgla_ref[...])
    gq_ref[...] = (pg[:, 0:GLA_KW] * (GLA_DK ** -0.5)).astype(BF16)
    gk_ref[...] = pg[:, GLA_KW:2 * GLA_KW].astype(BF16)
    gv_ref[...] = pg[:, 2 * GLA_KW:2 * GLA_KW + GLA_VW].astype(BF16)
    gg_ref[...] = pg[:, 2 * GLA_KW + GLA_VW:].astype(BF16)

    ga = _dot(hn, wga_ref[...]).astype(BF16)
    z = _dot(ga, wa2_ref[...]) + ba_ref[...]
    log_sig = jnp.minimum(z, 0.0) - jnp.log1p(jnp.exp(-jnp.abs(z)))
    la_ref[...] = log_sig * (1.0 / GLA_TAU)

    ps = _dot(hn, wswa_ref[...])
    cos = cos_ref[...]
    sin_signed = sin_ref[...]
    lane = lax.broadcasted_iota(jnp.int32, (1, LANES), 1)
    first_half = (lane % SWA_HEAD_DIM) < (SWA_HEAD_DIM // 2)
    sq = _rope(ps[:, 0:SWA_QW], cos, sin_signed, first_half)
    sq_ref[...] = (sq * (SWA_HEAD_DIM ** -0.5)).astype(BF16)
    sk_ref[...] = _rope(ps[:, SWA_QW:SWA_QW + SWA_KW], cos, sin_signed, first_half).astype(BF16)
    sv_ref[...] = ps[:, SWA_QW + SWA_KW:].astype(BF16)


def _proj_call(h, nw, wgla, wga, wa2, ba, wswa, cos, sin_signed, *, tm, seq_rows):
    rows = h.shape[0]
    seq_blocks = seq_rows // tm
    row_spec = lambda w: pl.BlockSpec((tm, w), lambda i: (i, 0))
    tab_spec = pl.BlockSpec((tm, LANES), lambda i: (i % seq_blocks, 0))
    widths = (GLA_KW, GLA_KW, GLA_VW, GLA_VW, GLA_KW, SWA_QW, SWA_KW, SWA_KW)
    dtypes = (BF16, BF16, BF16, BF16, F32, BF16, BF16, BF16)
    return pl.pallas_call(
        _proj_body,
        grid=(rows // tm,),
        in_specs=[
            row_spec(D_MODEL),
            _const_spec((1, D_MODEL)),
            _const_spec(wgla.shape),
            _const_spec(wga.shape),
            _const_spec(wa2.shape),
            _const_spec((1, GLA_KW)),
            _const_spec(wswa.shape),
            tab_spec,
            tab_spec,
        ],
        out_specs=[row_spec(w) for w in widths],
        out_shape=[jax.ShapeDtypeStruct((rows, w), dt) for w, dt in zip(widths, dtypes)],
        compiler_params=pltpu.CompilerParams(
            dimension_semantics=("arbitrary",), vmem_limit_bytes=VMEM_LIMIT),
        name=f"proj_tm{tm}",
    )(h, nw, wgla, wga, wa2, ba, wswa, cos, sin_signed)


_FAR_KEY_SLABS = ((0, 32, 31), (0, 8, 7), (0, 16, 15), (0, 24, 23), (32, 40, 39), (32, 48, 47), (32, 56, 55))
_FAR_QUERY_SLABS = ((32, 64), (8, 32), (40, 64))
_FAR_QUERY_REFS = ((31, 32), (7, 8), (15, 8), (23, 8), (39, 8), (47, 8), (55, 8))
_FAR_Q = sum(b - a for a, b in _FAR_QUERY_SLABS)
_FAR_K = sum(b - a for a, b, _ in _FAR_KEY_SLABS)


def _far_mask_np():
    m = np.zeros((_FAR_Q, _FAR_K), np.float32)
    m[0:32, 0:32] = 1.0
    col = 32
    row = 32
    for _half in range(2):
        for grp in range(1, 4):
            m[row:row + 8, col:col + 8 * grp] = 1.0
            row += 8
            col += 8 * grp
    return m


def _tri_blockdiag_np(rows):
    r = np.arange(rows)
    same = (r[:, None] // GLA_CHUNK) == (r[None, :] // GLA_CHUNK)
    return (same & (r[None, :] <= r[:, None])).astype(np.float32)


def _near_select_np():
    s = np.zeros((GLA_SUB, GLA_HEADS, GLA_DK, GLA_HEADS, GLA_CHUNK // GLA_SUB, GLA_SUB), np.float32)
    for jl in range(GLA_SUB):
        for h in range(GLA_HEADS):
            s[jl, h, :, h, :, jl] = 1.0
    return s.reshape(GLA_SUB * GLA_KW, GLA_KW)


def _cumsum_rows(tri, la):
    hi = la.astype(BF16)
    lo = (la - hi.astype(F32)).astype(BF16)
    return _dot(tri, hi) + _dot(tri, lo)


def _state_update(state_t, kc, vc_bf, bc, head_lane):
    b_last = bc[GLA_CHUNK - 1:GLA_CHUNK, :]
    k2 = (kc * jnp.exp(b_last - bc)).astype(BF16)
    u = _dot_tn(vc_bf, k2)
    upd = jnp.zeros((GLA_DV, GLA_KW), F32)
    for h in range(GLA_HEADS):
        upd = upd + jnp.where(head_lane[h], u[h * GLA_DV:(h + 1) * GLA_DV, :], 0.0)
    return state_t * jnp.exp(b_last) + upd


def _gla_chunk(qc, kc, vc, bc, near, state_t, head_lane, far_mask, vbd_mask):
    vc_bf = vc.astype(BF16)

    def ref_rows(r, n):
        return jnp.broadcast_to(bc[r:r + 1, :], (n, GLA_KW))

    qb = (qc * jnp.exp(bc)).astype(BF16)
    lhs = jnp.concatenate([jnp.where(head_lane[h], qb, 0) for h in range(GLA_HEADS)], axis=0)
    o_inter = _dot_nt(lhs, state_t.astype(BF16))

    q_rows = jnp.concatenate([qc[a:b] for a, b in _FAR_QUERY_SLABS], axis=0)
    bq = jnp.concatenate([bc[a:b] for a, b in _FAR_QUERY_SLABS], axis=0)
    rq = jnp.concatenate([ref_rows(r, n) for r, n in _FAR_QUERY_REFS], axis=0)
    q_far = (q_rows * jnp.exp(bq - rq)).astype(BF16)
    k_rows = jnp.concatenate([kc[a:b] for a, b, _ in _FAR_KEY_SLABS], axis=0)
    bk = jnp.concatenate([bc[a:b] for a, b, _ in _FAR_KEY_SLABS], axis=0)
    rk = jnp.concatenate([ref_rows(r, b - a) for a, b, r in _FAR_KEY_SLABS], axis=0)
    k_far = k_rows * jnp.exp(rk - bk)
    v_far = jnp.concatenate([vc[a:b] for a, b, _ in _FAR_KEY_SLABS], axis=0).astype(BF16)

    v_bd = jnp.where(vbd_mask, jnp.concatenate([vc_bf] * GLA_HEADS, axis=0), 0)
    o_near = _dot(near, v_bd)

    zeros8 = jnp.zeros((8, GLA_DV), F32)
    outs = []
    for h in range(GLA_HEADS):
        kh = jnp.where(head_lane[h], k_far, 0.0).astype(BF16)
        p = _dot_nt(q_far, kh)
        p = jnp.where(far_mask, p, 0.0).astype(BF16)
        of = _dot(p, v_far[:, h * GLA_DV:(h + 1) * GLA_DV])
        top = jnp.concatenate([zeros8, of[32:56]], axis=0)
        bot = of[0:32] + jnp.concatenate([zeros8, of[56:80]], axis=0)
        outs.append(jnp.concatenate([top, bot], axis=0) + o_inter[h * GLA_CHUNK:(h + 1) * GLA_CHUNK])
    o = jnp.concatenate(outs, axis=1) + o_near
    return o, _state_update(state_t, kc, vc_bf, bc, head_lane)


def _gla_body(q_ref, k_ref, v_ref, la_ref, km_ref, vm_ref, lam_ref, tri_ref, sel_ref, fmask_ref,
              o_ref, state_ref):
    lane = lax.broadcasted_iota(jnp.int32, (1, GLA_KW), 1)
    head_lane = [(lane // GLA_DK) == h for h in range(GLA_HEADS)]

    @pl.when(pl.program_id(1) == 0)
    def _():
        tri64 = tri_ref[0:GLA_CHUNK, 0:GLA_CHUNK]
        bm = _cumsum_rows(tri64, lam_ref[...])
        zero = jnp.zeros((GLA_DV, GLA_KW), F32)
        state_ref[...] = _state_update(zero, km_ref[...].astype(F32), vm_ref[...], bm, head_lane)

    q = q_ref[...].astype(F32)
    k = k_ref[...].astype(F32)
    v = v_ref[...].astype(F32)
    b = _cumsum_rows(tri_ref[...], la_ref[...])

    nblk = GLA_ROWS // GLA_SUB
    q3 = q.reshape(nblk, GLA_SUB, GLA_KW)
    k3 = k.reshape(nblk, GLA_SUB, GLA_KW)
    b3 = b.reshape(nblk, GLA_SUB, GLA_KW)
    sub = lax.broadcasted_iota(jnp.int32, (nblk, GLA_SUB, GLA_KW), 1)
    terms = []
    for jl in range(GLA_SUB):
        decay = jnp.exp(jnp.minimum(b3 - b3[:, jl:jl + 1, :], 0.0))
        t = jnp.where(sub >= jl, q3 * k3[:, jl:jl + 1, :] * decay, 0.0)
        terms.append(t.reshape(GLA_ROWS, GLA_KW).astype(BF16))
    near = _dot(jnp.concatenate(terms, axis=1), sel_ref[...])
    row = lax.broadcasted_iota(jnp.int32, (GLA_ROWS, GLA_KW), 0)
    col = lax.broadcasted_iota(jnp.int32, (GLA_ROWS, GLA_KW), 1)
    same_block = ((row // GLA_SUB) % (GLA_CHUNK // GLA_SUB)) == ((col // GLA_SUB) % (GLA_CHUNK // GLA_SUB))
    near = jnp.where(same_block, near, 0.0).astype(BF16)

    far_mask = fmask_ref[...] > 0.5
    vrow = lax.broadcasted_iota(jnp.int32, (GLA_HEADS * GLA_CHUNK, GLA_VW), 0)
    vcol = lax.broadcasted_iota(jnp.int32, (GLA_HEADS * GLA_CHUNK, GLA_VW), 1)
    vbd_mask = (vrow // GLA_CHUNK) == (vcol // GLA_DV)

    state_t = state_ref[...]
    for c in range(GLA_ROWS // GLA_CHUNK):
        rows = slice(c * GLA_CHUNK, (c + 1) * GLA_CHUNK)
        o, state_t = _gla_chunk(q[rows], k[rows], v[rows], b[rows], near[rows], state_t,
                                head_lane, far_mask, vbd_mask)
        o_ref[rows, :] = o.astype(BF16)
    state_ref[...] = state_t


def _gla_call(gq, gk, gv, la, km, vm, lam, *, batch, seq_rows):
    nsteps = seq_rows // GLA_ROWS
    row_spec = lambda w: pl.BlockSpec((GLA_ROWS, w), lambda b, i: (b * nsteps + i, 0))
    tri = jnp.asarray(_tri_blockdiag_np(GLA_ROWS), BF16)
    sel = jnp.asarray(_near_select_np(), BF16)
    fmask = jnp.asarray(_far_mask_np(), F32)
    return pl.pallas_call(
        _gla_body,
        grid=(batch, nsteps),
        in_specs=[
            row_spec(GLA_KW), row_spec(GLA_KW), row_spec(GLA_VW), row_spec(GLA_KW),
            _const_spec((GLA_CHUNK, GLA_KW)), _const_spec((GLA_CHUNK, GLA_VW)), _const_spec((GLA_CHUNK, GLA_KW)),
            _const_spec(tri.shape), _const_spec(sel.shape), _const_spec(fmask.shape),
        ],
        out_specs=row_spec(GLA_VW),
        out_shape=jax.ShapeDtypeStruct((batch * seq_rows, GLA_VW), BF16),
        scratch_shapes=[pltpu.VMEM((GLA_DV, GLA_KW), F32)],
        compiler_params=pltpu.CompilerParams(
            dimension_semantics=("arbitrary", "arbitrary"), vmem_limit_bytes=VMEM_LIMIT),
        name="gla",
    )(gq, gk, gv, la, km, vm, lam, tri, sel, fmask)


def _half_variants(x, low_lane):
    swapped = pltpu.roll(x, SWA_HEAD_DIM, axis=1)
    z = jnp.zeros_like(x)
    return (
        (jnp.where(low_lane, x, z).astype(BF16), jnp.where(low_lane, z, swapped).astype(BF16)),
        (jnp.where(low_lane, swapped, z).astype(BF16), jnp.where(low_lane, z, x).astype(BF16)),
    )


def _swa_body(sink_ref, q_ref, kp_ref, kc_ref, vp_ref, vc_ref, kmbd_ref, vmseg_ref, o_ref):
    first = pl.program_id(1) == 0
    t = SWA_BLOCK
    lane = lax.broadcasted_iota(jnp.int32, (1, LANES), 1)
    low_lane = lane < SWA_HEAD_DIM
    meta_head = lane // N_META
    k_var = _half_variants(jnp.concatenate([kp_ref[...], kc_ref[...]], axis=0).astype(F32), low_lane)
    v_var = _half_variants(jnp.concatenate([vp_ref[...], vc_ref[...]], axis=0).astype(F32), low_lane)
    ones = jnp.ones((2 * t, LANES), BF16)

    r = lax.broadcasted_iota(jnp.int32, (t, t), 0)
    c = lax.broadcasted_iota(jnp.int32, (t, t), 1)
    causal = c <= r
    not_causal = jnp.logical_not(causal)
    not_causal_first = jnp.logical_and(not_causal, jnp.logical_not(first))

    for j in range(SWA_QBLOCKS):
        qj = q_ref[j * t:(j + 1) * t, :]
        keys = slice(j * t, (j + 2) * t)
        use_prev = not_causal_first if j == 0 else not_causal
        sm = _dot(qj, kmbd_ref[...])
        pv = [None] * SWA_Q_HEADS
        m_head = [None] * SWA_Q_HEADS
        for g in range(SWA_KV_HEADS):
            lhs = jnp.concatenate([qj[:, 2 * g * LANES:(2 * g + 1) * LANES],
                                   qj[:, (2 * g + 1) * LANES:(2 * g + 2) * LANES]], axis=0)
            for half in range(2):
                s_all = _dot_nt(lhs, k_var[g][half][keys])
                rhs = jnp.concatenate([v_var[g][half][keys], ones], axis=1)
                for grp in range(2):
                    head = 4 * g + 2 * grp + half
                    s = s_all[grp * t:(grp + 1) * t]
                    s_sel = jnp.where(causal, s[:, t:2 * t], jnp.where(use_prev, s[:, 0:t], NEG_INF))
                    sm_h = jnp.where(meta_head == head, sm, NEG_INF)
                    m = jnp.max(jnp.maximum(s_sel, sm_h), axis=-1, keepdims=True)
                    m = jnp.maximum(m, sink_ref[0, head])
                    p = jnp.exp(s_sel - m)
                    p_band = jnp.concatenate([jnp.where(causal, 0.0, p), jnp.where(causal, p, 0.0)], axis=1)
                    pv[head] = _dot(p_band.astype(BF16), rhs)
                    m_head[head] = m
        m_meta = jnp.zeros((t, LANES), F32)
        for head in range(SWA_Q_HEADS):
            m_meta = jnp.where(meta_head == head, m_head[head], m_meta)
        pm = jnp.exp(sm - m_meta).astype(BF16)
        mo = _dot(pm, vmseg_ref[...])
        for grp in range(SWA_Q_HEADS // 2):
            he, ho = 2 * grp, 2 * grp + 1
            cols = slice(grp * LANES, (grp + 1) * LANES)
            num = pv[he][:, 0:LANES] + pv[ho][:, 0:LANES] + mo[:, cols]
            sink_term = jnp.where(low_lane, jnp.exp(sink_ref[0, he] - m_head[he]),
                                  jnp.exp(sink_ref[0, ho] - m_head[ho]))
            den = (jnp.where(low_lane, pv[he][:, LANES:], pv[ho][:, LANES:])
                   + mo[:, SWA_QW + grp * LANES:SWA_QW + (grp + 1) * LANES] + sink_term)
            o_ref[j * t:(j + 1) * t, cols] = (num / den).astype(BF16)


def _swa_meta_operands(sk_m, sv_m):
    eye = jnp.eye(SWA_Q_HEADS, dtype=F32)
    per_head = lambda a: jnp.stack(
        [a[:, (h // 4) * SWA_HEAD_DIM:(h // 4 + 1) * SWA_HEAD_DIM] for h in range(SWA_Q_HEADS)]).astype(F32)
    km = per_head(sk_m)
    vm = per_head(sv_m)
    kmbd = jnp.einsum('hmd,hk->hdkm', km, eye).reshape(SWA_QW, SWA_Q_HEADS * N_META)
    vmbd = jnp.einsum('hmd,hk->hmkd', vm, eye).reshape(SWA_Q_HEADS * N_META, SWA_QW)
    seg = jnp.kron(eye, jnp.ones((N_META, SWA_HEAD_DIM), F32))
    return kmbd.astype(BF16), jnp.concatenate([vmbd, seg], axis=1).astype(BF16)


def _swa_call(sq, sk, sv, kmbd, vmseg, sinks, *, batch, seq_rows):
    nb = seq_rows // SWA_BLOCK
    step = SWA_QBLOCKS * SWA_BLOCK
    nsteps = seq_rows // step
    cur = lambda w: pl.BlockSpec((step, w), lambda b, i: (b * nsteps + i, 0))
    prev = lambda w: pl.BlockSpec(
        (SWA_BLOCK, w), lambda b, i: (b * nb + jnp.maximum(i * SWA_QBLOCKS - 1, 0), 0))
    return pl.pallas_call(
        _swa_body,
        grid=(batch, nsteps),
        in_specs=[
            pl.BlockSpec(memory_space=pltpu.SMEM),
            cur(SWA_QW), prev(SWA_KW), cur(SWA_KW), prev(SWA_KW), cur(SWA_KW),
            _const_spec(kmbd.shape), _const_spec(vmseg.shape),
        ],
        out_specs=cur(SWA_QW),
        out_shape=jax.ShapeDtypeStruct((batch * seq_rows, SWA_QW), BF16),
        compiler_params=pltpu.CompilerParams(
            dimension_semantics=("arbitrary", "arbitrary"), vmem_limit_bytes=VMEM_LIMIT),
        name="swa",
    )(sinks, sq, sk, sk, sv, sv, kmbd, vmseg)


def _out_body(og_ref, gg_ref, os_ref, h_ref, gnw_ref, snw_ref, wout_ref, pnw_ref, o_ref):
    og = og_ref[...].astype(F32)
    gg = gg_ref[...].astype(F32)
    gnw = gnw_ref[...]
    parts = []
    for h in range(GLA_HEADS):
        cols = slice(h * GLA_DV, (h + 1) * GLA_DV)
        gate = gg[:, cols]
        parts.append(_rms(og[:, cols], gnw) * (gate * jax.nn.sigmoid(gate)))
    parts.append(_rms(os_ref[...].astype(F32), snw_ref[...]))
    mixed = jnp.concatenate(parts, axis=1).astype(BF16)
    m = _dot(mixed, wout_ref[...])
    o_ref[...] = h_ref[...] + _rms(m, pnw_ref[...])


def _out_call(og, gg, osw, h, gnw, snw, wout, pnw, *, tm):
    rows = h.shape[0]
    row_spec = lambda w: pl.BlockSpec((tm, w), lambda i: (i, 0))
    return pl.pallas_call(
        _out_body,
        grid=(rows // tm,),
        in_specs=[
            row_spec(GLA_VW), row_spec(GLA_VW), row_spec(SWA_QW), row_spec(D_MODEL),
            _const_spec((1, GLA_DV)), _const_spec((1, SWA_QW)), _const_spec((D_MODEL, D_MODEL)),
            _const_spec((1, D_MODEL)),
        ],
        out_specs=row_spec(D_MODEL),
        out_shape=jax.ShapeDtypeStruct((rows, D_MODEL), F32),
        compiler_params=pltpu.CompilerParams(
            dimension_semantics=("arbitrary",), vmem_limit_bytes=VMEM_LIMIT),
        name="mix_out",
    )(og, gg, osw, h, gnw, snw, wout, pnw)


def _rope_tables(positions):
    inv_freq = 1.0 / (ROPE_THETA ** (jnp.arange(0, SWA_HEAD_DIM, 2, dtype=F32) / SWA_HEAD_DIM))
    ang = positions.astype(F32)[:, None] * inv_freq[None, :]
    ang = jnp.concatenate([ang, ang, ang, ang], axis=-1)
    sign = jnp.where((jnp.arange(LANES) % SWA_HEAD_DIM) < SWA_HEAD_DIM // 2, -1.0, 1.0).astype(F32)
    return jnp.cos(ang), jnp.sin(ang) * sign[None, :]


def kernel(x, meta_tokens, ffn1_pre_norm, ffn1_w_gate, ffn1_w_up, ffn1_w_down, ffn1_post_norm, mix_pre_norm, w_in, gla_w_a2, gla_b_a, gla_out_norm, swa_sinks, swa_out_norm, w_out, mix_post_norm, ffn2_pre_norm, ffn2_w_gate, ffn2_w_up, ffn2_w_down, ffn2_post_norm):
    batch, seq, _ = x.shape
    depth = ffn1_pre_norm.shape[0]
    assert depth == 1 and seq % GLA_ROWS == 0
    row = lambda a: a.reshape(1, -1).astype(F32)
    tm = 512

    w = w_in[0]
    n_gla = 2 * GLA_KW + 2 * GLA_VW
    wgla = w[:, :n_gla].astype(BF16)
    wga = jnp.pad(w[:, n_gla:n_gla + N_META], ((0, 0), (0, GA_PAD - N_META))).astype(BF16)
    wswa = w[:, n_gla + N_META:].astype(BF16)
    wa2 = jnp.pad(gla_w_a2[0], ((0, GA_PAD - N_META), (0, 0))).astype(BF16)
    ba = row(gla_b_a[0])

    ffn1 = (row(ffn1_pre_norm[0]), ffn1_w_gate[0].astype(BF16), ffn1_w_up[0].astype(BF16),
            ffn1_w_down[0].astype(BF16), row(ffn1_post_norm[0]))
    ffn2 = (row(ffn2_pre_norm[0]), ffn2_w_gate[0].astype(BF16), ffn2_w_up[0].astype(BF16),
            ffn2_w_down[0].astype(BF16), row(ffn2_post_norm[0]))
    proj_w = (row(mix_pre_norm[0]), wgla, wga, wa2, ba, wswa)

    hm = _ffn_call(meta_tokens.astype(F32), *ffn1, tm=N_META)
    cos_m, sin_m = _rope_tables(jnp.arange(N_META))
    _, gk_m, gv_m, _, la_m, _, sk_m, sv_m = _proj_call(hm, *proj_w, cos_m, sin_m, tm=N_META, seq_rows=N_META)
    front = GLA_CHUNK - N_META
    pad_front = lambda a: jnp.pad(a, ((front, 0), (0, 0)))

    xs = x.reshape(batch * seq, D_MODEL)
    h1 = _ffn_call(xs, *ffn1, tm=tm)
    cos_s, sin_s = _rope_tables(N_META + jnp.arange(seq))
    gq, gk, gv, gg, la, sq, sk, sv = _proj_call(h1, *proj_w, cos_s, sin_s, tm=tm, seq_rows=seq)
    o_gla = _gla_call(gq, gk, gv, la, pad_front(gk_m), pad_front(gv_m), pad_front(la_m),
                      batch=batch, seq_rows=seq)
    kmbd, vmseg = _swa_meta_operands(sk_m, sv_m)
    o_swa = _swa_call(sq, sk, sv, kmbd, vmseg, swa_sinks.reshape(1, SWA_Q_HEADS).astype(F32),
                      batch=batch, seq_rows=seq)
    h2 = _out_call(o_gla, gg, o_swa, h1, row(gla_out_norm[0]), row(swa_out_norm[0]),
                   w_out[0].astype(BF16), row(mix_post_norm[0]), tm=tm)
    out = _ffn_call(h2, *ffn2, tm=tm)
    return out.reshape(batch, seq, D_MODEL)
```

```python
import functools

import numpy as np
import jax
import jax.numpy as jnp
from jax import lax
from jax.experimental import pallas as pl
from jax.experimental.pallas import tpu as pltpu

F32 = jnp.float32
BF16 = jnp.bfloat16

D_MODEL = 1024
D_FF = 2816
N_META = 16
NORM_EPS = 1e-6
NEG_INF = -1e30
ROPE_THETA = 10000.0

GLA_HEADS = 4
GLA_DK = 64
GLA_DV = 128
GLA_KW = GLA_HEADS * GLA_DK
GLA_VW = GLA_HEADS * GLA_DV
GLA_TAU = 16.0
GLA_CHUNK = 64
GLA_SUB = 8
GLA_ROWS = 512
LOG2E = 1.4426950408889634

SWA_HEAD_DIM = 64
SWA_Q_HEADS = 8
SWA_KV_HEADS = 2
SWA_QW = SWA_Q_HEADS * SWA_HEAD_DIM
SWA_KW = SWA_KV_HEADS * SWA_HEAD_DIM
SWA_BLOCK = 128
SWA_QBLOCKS = 4

LANES = 128
GA_PAD = LANES
FF_CHUNK = 256

VMEM_LIMIT = 56 * 1024 * 1024


def _dot(a, b):
    return jnp.dot(a, b, preferred_element_type=F32)


def _dot_nt(a, b):
    return lax.dot_general(a, b, (((1,), (1,)), ((), ())), preferred_element_type=F32)


def _dot_tn(a, b):
    return lax.dot_general(a, b, (((0,), (0,)), ((), ())), preferred_element_type=F32)


def _rms(x, w):
    ms = jnp.mean(x * x, axis=-1, keepdims=True)
    return x * lax.rsqrt(ms + NORM_EPS) * w


def _const_spec(shape):
    return pl.BlockSpec(shape, lambda *_: (0,) * len(shape), pipeline_mode=pl.Buffered(1))


def _ffn_body(x_ref, pre_ref, wg_ref, wu_ref, wd_ref, post_ref, o_ref, act_ref):
    x = x_ref[...]
    xn = _rms(x, pre_ref[...])
    for c in range(D_FF // FF_CHUNK):
        cols = slice(c * FF_CHUNK, (c + 1) * FF_CHUNK)
        g = _dot(xn, wg_ref[:, cols])
        u = _dot(xn, wu_ref[:, cols])
        act_ref[:, cols] = g * jax.nn.sigmoid(g) * u
    f = _dot(act_ref[...], wd_ref[...])
    o_ref[...] = x + 0.5 * _rms(f, post_ref[...])


def _ffn_call(x, pre, wg, wu, wd, post, *, tm):
    rows = x.shape[0]
    return pl.pallas_call(
        _ffn_body,
        grid=(rows // tm,),
        in_specs=[
            pl.BlockSpec((tm, D_MODEL), lambda i: (i, 0)),
            _const_spec((1, D_MODEL)),
            _const_spec((D_MODEL, D_FF)),
            _const_spec((D_MODEL, D_FF)),
            _const_spec((D_FF, D_MODEL)),
            _const_spec((1, D_MODEL)),
        ],
        out_specs=pl.BlockSpec((tm, D_MODEL), lambda i: (i, 0)),
        out_shape=jax.ShapeDtypeStruct((rows, D_MODEL), F32),
        scratch_shapes=[pltpu.VMEM((tm, D_FF), F32)],
        compiler_params=pltpu.CompilerParams(
            dimension_semantics=("arbitrary",), vmem_limit_bytes=VMEM_LIMIT),
        name=f"ffn_tm{tm}",
    )(x, pre, wg, wu, wd, post)


def _rope(x, cos, sin_signed, first_half):
    out = []
    for g in range(x.shape[1] // LANES):
        xg = x[:, g * LANES:(g + 1) * LANES]
        fwd = pltpu.roll(xg, LANES - 32, axis=1)
        bwd = pltpu.roll(xg, 32, axis=1)
        out.append(xg * cos + jnp.where(first_half, fwd, bwd) * sin_signed)
    return out[0] if len(out) == 1 else jnp.concatenate(out, axis=1)


def _proj_body(h_ref, nw_ref, wgla_ref, wga_ref, wa2_ref, ba_ref, wswa_ref, cos_ref, sin_ref,
               gq_ref, gk_ref, gv_ref, gg_ref, la_ref, sq_ref, sk_ref, sv_ref):
    hn = _rms(h_ref[...], nw_ref[...])
    pg = _dot(hn, wgla_ref[...])
    gq_ref[...] = (pg[:, 0:GLA_KW] * (GLA_DK ** -0.5)).astype(BF16)
    gk_ref[...] = pg[:, GLA_KW:2 * GLA_KW].astype(BF16)
    gv_ref[...] = pg[:, 2 * GLA_KW:2 * GLA_KW + GLA_VW].astype(BF16)
    gg_ref[...] = pg[:, 2 * GLA_KW + GLA_VW:].astype(BF16)

    ga = _dot(hn, wga_ref[...])
    z = _dot(ga, wa2_ref[...]) + ba_ref[...]
    log_sig = jnp.minimum(z, 0.0) - jnp.log1p(jnp.exp(-jnp.abs(z)))
    la_ref[...] = log_sig * (1.0 / GLA_TAU)

    ps = _dot(hn, wswa_ref[...])
    cos = cos_ref[...]
    sin_signed = sin_ref[...]
    lane = lax.broadcasted_iota(jnp.int32, (1, LANES), 1)
    first_half = (lane % SWA_HEAD_DIM) < (SWA_HEAD_DIM // 2)
    sq = _rope(ps[:, 0:SWA_QW], cos, sin_signed, first_half)
    sq_ref[...] = (sq * (SWA_HEAD_DIM ** -0.5)).astype(BF16)
    sk_ref[...] = _rope(ps[:, SWA_QW:SWA_QW + SWA_KW], cos, sin_signed, first_half).astype(BF16)
    sv_ref[...] = ps[:, SWA_QW + SWA_KW:].astype(BF16)


def _proj_call(h, nw, wgla, wga, wa2, ba, wswa, cos, sin_signed, *, tm, seq_rows):
    rows = h.shape[0]
    seq_blocks = seq_rows // tm
    row_spec = lambda w: pl.BlockSpec((tm, w), lambda i: (i, 0))
    tab_spec = pl.BlockSpec((tm, LANES), lambda i: (i % seq_blocks, 0))
    widths = (GLA_KW, GLA_KW, GLA_VW, GLA_VW, GLA_KW, SWA_QW, SWA_KW, SWA_KW)
    dtypes = (BF16, BF16, BF16, BF16, F32, BF16, BF16, BF16)
    return pl.pallas_call(
        _proj_body,
        grid=(rows // tm,),
        in_specs=[
            row_spec(D_MODEL),
            _const_spec((1, D_MODEL)),
            _const_spec((D_MODEL, 2 * GLA_KW + 2 * GLA_VW)),
            _const_spec(wga.shape),
            _const_spec(wa2.shape),
            _const_spec((1, GLA_KW)),
            _const_spec(wswa.shape),
            tab_spec,
            tab_spec,
        ],
        out_specs=[row_spec(w) for w in widths],
        out_shape=[jax.ShapeDtypeStruct((rows, w), dt) for w, dt in zip(widths, dtypes)],
        compiler_params=pltpu.CompilerParams(
            dimension_semantics=("arbitrary",), vmem_limit_bytes=VMEM_LIMIT),
        name=f"proj_tm{tm}",
    )(h, nw, wgla, wga, wa2, ba, wswa, cos, sin_signed)


_FAR_KEY_SLABS = ((0, 32, 31), (0, 8, 7), (0, 16, 15), (0, 24, 23), (32, 40, 39), (32, 48, 47), (32, 56, 55))
_FAR_QUERY_SLABS = ((32, 64), (8, 32), (40, 64))
_FAR_QUERY_REFS = ((31, 32), (7, 8), (15, 8), (23, 8), (39, 8), (47, 8), (55, 8))
_FAR_Q = sum(b - a for a, b in _FAR_QUERY_SLABS)
_FAR_K = sum(b - a for a, b, _ in _FAR_KEY_SLABS)


def _far_mask_np():
    m = np.zeros((_FAR_Q, _FAR_K), np.float32)
    m[0:32, 0:32] = 1.0
    col = 32
    row = 32
    for _half in range(2):
        for grp in range(1, 4):
            m[row:row + 8, col:col + 8 * grp] = 1.0
            row += 8
            col += 8 * grp
    return m


def _near_select_np():
    s = np.zeros((GLA_SUB, GLA_HEADS, GLA_DK, GLA_HEADS, GLA_CHUNK // GLA_SUB, GLA_SUB), np.float32)
    for jl in range(GLA_SUB):
        for h in range(GLA_HEADS):
            s[jl, h, :, h, :, jl] = 1.0
    return s.reshape(GLA_SUB * GLA_KW, GLA_KW)


def _cumsum_chunks(x):
    rows = x.shape[0]
    x3 = x.reshape(rows // GLA_SUB, GLA_SUB, GLA_KW)
    sub = lax.broadcasted_iota(jnp.int32, x3.shape, 1)
    for s in (1, 2, 4):
        x3 = x3 + jnp.where(sub >= s, pltpu.roll(x3, s, axis=1), 0.0)
    x = x3.reshape(rows, GLA_KW)
    out = []
    for c in range(rows // GLA_CHUNK):
        carry = None
        for t in range(GLA_CHUNK // GLA_SUB):
            lo = c * GLA_CHUNK + t * GLA_SUB
            blk = x[lo:lo + GLA_SUB]
            if carry is not None:
                blk = blk + carry
            out.append(blk)
            carry = blk[GLA_SUB - 1:GLA_SUB, :]
    return jnp.concatenate(out, axis=0)


def _state_update(state_t, kc, vc_bf, bc, head_lane):
    b_last = bc[GLA_CHUNK - 1:GLA_CHUNK, :]
    k2 = (kc * jnp.exp2(b_last - bc)).astype(BF16)
    u = _dot_tn(vc_bf, k2)
    upd = jnp.zeros((GLA_DV, GLA_KW), F32)
    for h in range(GLA_HEADS):
        upd = upd + jnp.where(head_lane[h], u[h * GLA_DV:(h + 1) * GLA_DV, :], 0.0)
    return state_t * jnp.exp2(b_last) + upd


def _gla_chunk(qc, kc, vc, bc, near, state_t, head_lane, far_mask, kbd_mask, vbd_mask):
    vc_bf = vc.astype(BF16)

    def ref_rows(r, n):
        return jnp.broadcast_to(bc[r:r + 1, :], (n, GLA_KW))

    qb = (qc * jnp.exp2(bc)).astype(BF16)
    lhs = jnp.concatenate([jnp.where(head_lane[h], qb, 0) for h in range(GLA_HEADS)], axis=0)
    o_inter = _dot_nt(lhs, state_t.astype(BF16))

    q_rows = jnp.concatenate([qc[a:b] for a, b in _FAR_QUERY_SLABS], axis=0)
    bq = jnp.concatenate([bc[a:b] for a, b in _FAR_QUERY_SLABS], axis=0)
    rq = jnp.concatenate([ref_rows(r, n) for r, n in _FAR_QUERY_REFS], axis=0)
    q_far = (q_rows * jnp.exp2(bq - rq)).astype(BF16)
    k_rows = jnp.concatenate([kc[a:b] for a, b, _ in _FAR_KEY_SLABS], axis=0)
    bk = jnp.concatenate([bc[a:b] for a, b, _ in _FAR_KEY_SLABS], axis=0)
    rk = jnp.concatenate([ref_rows(r, b - a) for a, b, r in _FAR_KEY_SLABS], axis=0)
    k_far_t = (k_rows * jnp.exp2(rk - bk)).T.astype(BF16)
    k_bd = jnp.where(kbd_mask, jnp.concatenate([k_far_t] * GLA_HEADS, axis=1), 0)
    p = _dot(q_far, k_bd)
    p = jnp.where(far_mask, p, 0.0).astype(BF16)
    v_far = jnp.concatenate([vc[a:b] for a, b, _ in _FAR_KEY_SLABS], axis=0).astype(BF16)

    v_bd = jnp.where(vbd_mask, jnp.concatenate([vc_bf] * GLA_HEADS, axis=0), 0)
    o_near = _dot(near, v_bd)

    zeros8 = jnp.zeros((8, GLA_DV), F32)
    outs = []
    for h in range(GLA_HEADS):
        cols = slice(h * GLA_DV, (h + 1) * GLA_DV)
        of = _dot(p[:, cols], v_far[:, cols])
        top = jnp.concatenate([zeros8, of[32:56]], axis=0)
        bot = of[0:32] + jnp.concatenate([zeros8, of[56:80]], axis=0)
        outs.append(jnp.concatenate([top, bot], axis=0) + o_inter[h * GLA_CHUNK:(h + 1) * GLA_CHUNK])
    o = jnp.concatenate(outs, axis=1) + o_near
    return o, _state_update(state_t, kc, vc_bf, bc, head_lane)


def _gla_body(q_ref, k_ref, v_ref, la_ref, km_ref, vm_ref, lam_ref, sel_ref, fmask_ref,
              o_ref, state_ref):
    lane = lax.broadcasted_iota(jnp.int32, (1, GLA_KW), 1)
    head_lane = [(lane // GLA_DK) == h for h in range(GLA_HEADS)]

    @pl.when(pl.program_id(1) == 0)
    def _():
        bm = _cumsum_chunks(lam_ref[...] * LOG2E)
        zero = jnp.zeros((GLA_DV, GLA_KW), F32)
        state_ref[...] = _state_update(zero, km_ref[...].astype(F32), vm_ref[...], bm, head_lane)

    q = q_ref[...].astype(F32)
    k = k_ref[...].astype(F32)
    v = v_ref[...].astype(F32)
    b = _cumsum_chunks(la_ref[...] * LOG2E)

    nblk = GLA_ROWS // GLA_SUB
    q3 = q.reshape(nblk, GLA_SUB, GLA_KW)
    k3 = k.reshape(nblk, GLA_SUB, GLA_KW)
    b3 = b.reshape(nblk, GLA_SUB, GLA_KW)
    terms = []
    for jl in range(GLA_SUB):
        decay = jnp.exp2(jnp.minimum(b3 - b3[:, jl:jl + 1, :], 0.0))
        t = q3 * k3[:, jl:jl + 1, :] * decay
        terms.append(t.reshape(GLA_ROWS, GLA_KW).astype(BF16))
    near = _dot(jnp.concatenate(terms, axis=1), sel_ref[...])
    row = lax.broadcasted_iota(jnp.int32, (GLA_ROWS, GLA_KW), 0)
    col = lax.broadcasted_iota(jnp.int32, (GLA_ROWS, GLA_KW), 1)
    blocks = GLA_CHUNK // GLA_SUB
    near_mask = jnp.logical_and((row // GLA_SUB) % blocks == (col // GLA_SUB) % blocks,
                                col % GLA_SUB <= row % GLA_SUB)
    near = jnp.where(near_mask, near, 0.0).astype(BF16)

    far_mask = fmask_ref[...] > 0.5
    krow = lax.broadcasted_iota(jnp.int32, (GLA_KW, GLA_HEADS * _FAR_K), 0)
    kcol = lax.broadcasted_iota(jnp.int32, (GLA_KW, GLA_HEADS * _FAR_K), 1)
    kbd_mask = (krow // GLA_DK) == (kcol // _FAR_K)
    vrow = lax.broadcasted_iota(jnp.int32, (GLA_HEADS * GLA_CHUNK, GLA_VW), 0)
    vcol = lax.broadcasted_iota(jnp.int32, (GLA_HEADS * GLA_CHUNK, GLA_VW), 1)
    vbd_mask = (vrow // GLA_CHUNK) == (vcol // GLA_DV)

    state_t = state_ref[...]
    for c in range(GLA_ROWS // GLA_CHUNK):
        rows = slice(c * GLA_CHUNK, (c + 1) * GLA_CHUNK)
        o, state_t = _gla_chunk(q[rows], k[rows], v[rows], b[rows], near[rows], state_t,
                                head_lane, far_mask, kbd_mask, vbd_mask)
        o_ref[rows, :] = o.astype(BF16)
    state_ref[...] = state_t


def _gla_call(gq, gk, gv, la, km, vm, lam, *, batch, seq_rows):
    nsteps = seq_rows // GLA_ROWS
    row_spec = lambda w: pl.BlockSpec((GLA_ROWS, w), lambda b, i: (b * nsteps + i, 0))
    sel = jnp.asarray(_near_select_np(), BF16)
    fmask = jnp.asarray(np.tile(_far_mask_np(), (1, GLA_HEADS)), F32)
    return pl.pallas_call(
        _gla_body,
        grid=(batch, nsteps),
        in_specs=[
            row_spec(GLA_KW), row_spec(GLA_KW), row_spec(GLA_VW), row_spec(GLA_KW),
            _const_spec((GLA_CHUNK, GLA_KW)), _const_spec((GLA_CHUNK, GLA_VW)), _const_spec((GLA_CHUNK, GLA_KW)),
            _const_spec(sel.shape), _const_spec(fmask.shape),
        ],
        out_specs=row_spec(GLA_VW),
        out_shape=jax.ShapeDtypeStruct((batch * seq_rows, GLA_VW), BF16),
        scratch_shapes=[pltpu.VMEM((GLA_DV, GLA_KW), F32)],
        compiler_params=pltpu.CompilerParams(
            dimension_semantics=("arbitrary", "arbitrary"), vmem_limit_bytes=VMEM_LIMIT),
        name="gla",
    )(gq, gk, gv, la, km, vm, lam, sel, fmask)


def _half_variants(x, low_lane):
    swapped = pltpu.roll(x, SWA_HEAD_DIM, axis=1)
    z = jnp.zeros_like(x)
    return (
        (jnp.where(low_lane, x, z).astype(BF16), jnp.where(low_lane, z, swapped).astype(BF16)),
        (jnp.where(low_lane, swapped, z).astype(BF16), jnp.where(low_lane, z, x).astype(BF16)),
    )


def _swa_body(sink_ref, q_ref, kp_ref, kc_ref, vp_ref, vc_ref, kmbd_ref, vmseg_ref, o_ref):
    first = pl.program_id(1) == 0
    t = SWA_BLOCK
    lane = lax.broadcasted_iota(jnp.int32, (1, LANES), 1)
    low_lane = lane < SWA_HEAD_DIM
    meta_head = lane // N_META
    k_var = _half_variants(jnp.concatenate([kp_ref[...], kc_ref[...]], axis=0).astype(F32), low_lane)
    v_var = _half_variants(jnp.concatenate([vp_ref[...], vc_ref[...]], axis=0).astype(F32), low_lane)
    ones = jnp.ones((2 * t, LANES), BF16)

    r = lax.broadcasted_iota(jnp.int32, (t, t), 0)
    c = lax.broadcasted_iota(jnp.int32, (t, t), 1)
    causal = c <= r
    not_causal = jnp.logical_not(causal)
    not_causal_first = jnp.logical_and(not_causal, jnp.logical_not(first))

    for j in range(SWA_QBLOCKS):
        qj = q_ref[j * t:(j + 1) * t, :]
        keys = slice(j * t, (j + 2) * t)
        use_prev = not_causal_first if j == 0 else not_causal
        sm = _dot(qj, kmbd_ref[...])
        pv = [None] * SWA_Q_HEADS
        m_head = [None] * SWA_Q_HEADS
        for g in range(SWA_KV_HEADS):
            lhs = jnp.concatenate([qj[:, 2 * g * LANES:(2 * g + 1) * LANES],
                                   qj[:, (2 * g + 1) * LANES:(2 * g + 2) * LANES]], axis=0)
            for half in range(2):
                s_all = _dot_nt(lhs, k_var[g][half][keys])
                rhs = jnp.concatenate([v_var[g][half][keys], ones], axis=1)
                for grp in range(2):
                    head = 4 * g + 2 * grp + half
                    s = s_all[grp * t:(grp + 1) * t]
                    s_sel = jnp.where(causal, s[:, t:2 * t], jnp.where(use_prev, s[:, 0:t], NEG_INF))
                    sm_h = jnp.where(meta_head == head, sm, NEG_INF)
                    m = jnp.max(jnp.maximum(s_sel, sm_h), axis=-1, keepdims=True)
                    m = jnp.maximum(m, sink_ref[0, head])
                    p = jnp.exp(s_sel - m)
                    p_band = jnp.concatenate([jnp.where(causal, 0.0, p), jnp.where(causal, p, 0.0)], axis=1)
                    pv[head] = _dot(p_band.astype(BF16), rhs)
                    m_head[head] = m
        m_meta = jnp.zeros((t, LANES), F32)
        for head in range(SWA_Q_HEADS):
            m_meta = jnp.where(meta_head == head, m_head[head], m_meta)
        pm = jnp.exp(sm - m_meta).astype(BF16)
        mo = _dot(pm, vmseg_ref[...])
        for grp in range(SWA_Q_HEADS // 2):
            he, ho = 2 * grp, 2 * grp + 1
            cols = slice(grp * LANES, (grp + 1) * LANES)
            num = pv[he][:, 0:LANES] + pv[ho][:, 0:LANES] + mo[:, cols]
            sink_term = jnp.where(low_lane, jnp.exp(sink_ref[0, he] - m_head[he]),
                                  jnp.exp(sink_ref[0, ho] - m_head[ho]))
            den = (jnp.where(low_lane, pv[he][:, LANES:], pv[ho][:, LANES:])
                   + mo[:, SWA_QW + grp * LANES:SWA_QW + (grp + 1) * LANES] + sink_term)
            o_ref[j * t:(j + 1) * t, cols] = (num / den).astype(BF16)


def _swa_meta_operands(sk_m, sv_m):
    eye = jnp.eye(SWA_Q_HEADS, dtype=F32)
    per_head = lambda a: jnp.stack(
        [a[:, (h // 4) * SWA_HEAD_DIM:(h // 4 + 1) * SWA_HEAD_DIM] for h in range(SWA_Q_HEADS)]).astype(F32)
    km = per_head(sk_m)
    vm = per_head(sv_m)
    kmbd = jnp.einsum('hmd,hk->hdkm', km, eye).reshape(SWA_QW, SWA_Q_HEADS * N_META)
    vmbd = jnp.einsum('hmd,hk->hmkd', vm, eye).reshape(SWA_Q_HEADS * N_META, SWA_QW)
    seg = jnp.kron(eye, jnp.ones((N_META, SWA_HEAD_DIM), F32))
    return kmbd.astype(BF16), jnp.concatenate([vmbd, seg], axis=1).astype(BF16)


def _swa_call(sq, sk, sv, kmbd, vmseg, sinks, *, batch, seq_rows):
    nb = seq_rows // SWA_BLOCK
    step = SWA_QBLOCKS * SWA_BLOCK
    nsteps = seq_rows // step
    cur = lambda w: pl.BlockSpec((step, w), lambda b, i: (b * nsteps + i, 0))
    prev = lambda w: pl.BlockSpec(
        (SWA_BLOCK, w), lambda b, i: (b * nb + jnp.maximum(i * SWA_QBLOCKS - 1, 0), 0))
    return pl.pallas_call(
        _swa_body,
        grid=(batch, nsteps),
        in_specs=[
            pl.BlockSpec(memory_space=pltpu.SMEM),
            cur(SWA_QW), prev(SWA_KW), cur(SWA_KW), prev(SWA_KW), cur(SWA_KW),
            _const_spec(kmbd.shape), _const_spec(vmseg.shape),
        ],
        out_specs=cur(SWA_QW),
        out_shape=jax.ShapeDtypeStruct((batch * seq_rows, SWA_QW), BF16),
        compiler_params=pltpu.CompilerParams(
            dimension_semantics=("arbitrary", "arbitrary"), vmem_limit_bytes=VMEM_LIMIT),
        name="swa",
    )(sinks, sq, sk, sk, sv, sv, kmbd, vmseg)


def _out_body(og_ref, gg_ref, os_ref, h_ref, gnw_ref, snw_ref, wout_ref, pnw_ref, o_ref):
    og = og_ref[...].astype(F32)
    gg = gg_ref[...].astype(F32)
    gnw = gnw_ref[...]
    parts = []
    for h in range(GLA_HEADS):
        cols = slice(h * GLA_DV, (h + 1) * GLA_DV)
        gate = gg[:, cols]
        parts.append(_rms(og[:, cols], gnw) * (gate * jax.nn.sigmoid(gate)))
    parts.append(_rms(os_ref[...].astype(F32), snw_ref[...]))
    mixed = jnp.concatenate(parts, axis=1)
    m = _dot(mixed, wout_ref[...])
    o_ref[...] = h_ref[...] + _rms(m, pnw_ref[...])


def _out_call(og, gg, osw, h, gnw, snw, wout, pnw, *, tm):
    rows = h.shape[0]
    row_spec = lambda w: pl.BlockSpec((tm, w), lambda i: (i, 0))
    return pl.pallas_call(
        _out_body,
        grid=(rows // tm,),
        in_specs=[
            row_spec(GLA_VW), row_spec(GLA_VW), row_spec(SWA_QW), row_spec(D_MODEL),
            _const_spec((1, GLA_DV)), _const_spec((1, SWA_QW)), _const_spec((D_MODEL, D_MODEL)),
            _const_spec((1, D_MODEL)),
        ],
        out_specs=row_spec(D_MODEL),
        out_shape=jax.ShapeDtypeStruct((rows, D_MODEL), F32),
        compiler_params=pltpu.CompilerParams(
            dimension_semantics=("arbitrary",), vmem_limit_bytes=VMEM_LIMIT),
        name="mix_out",
    )(og, gg, osw, h, gnw, snw, wout, pnw)


def _rope_tables(positions):
    inv_freq = 1.0 / (ROPE_THETA ** (jnp.arange(0, SWA_HEAD_DIM, 2, dtype=F32) / SWA_HEAD_DIM))
    ang = positions.astype(F32)[:, None] * inv_freq[None, :]
    cos, sin = jnp.cos(ang), jnp.sin(ang)
    return jnp.concatenate([cos, cos, cos, cos], axis=-1), jnp.concatenate([-sin, sin, -sin, sin], axis=-1)


def kernel(x, meta_tokens, ffn1_pre_norm, ffn1_w_gate, ffn1_w_up, ffn1_w_down, ffn1_post_norm, mix_pre_norm, w_in, gla_w_a2, gla_b_a, gla_out_norm, swa_sinks, swa_out_norm, w_out, mix_post_norm, ffn2_pre_norm, ffn2_w_gate, ffn2_w_up, ffn2_w_down, ffn2_post_norm):
    batch, seq, _ = x.shape
    depth = ffn1_pre_norm.shape[0]
    assert depth == 1 and seq % GLA_ROWS == 0
    row = lambda a: a.reshape(1, -1).astype(F32)
    tm = 512

    w = w_in[0]
    n_gla = 2 * GLA_KW + 2 * GLA_VW
    wga = jnp.pad(w[:, n_gla:n_gla + N_META], ((0, 0), (0, GA_PAD - N_META)))
    wswa = w[:, n_gla + N_META:]
    wa2 = jnp.pad(gla_w_a2[0], ((0, GA_PAD - N_META), (0, 0)))
    ba = row(gla_b_a[0])

    ffn1 = (row(ffn1_pre_norm[0]), ffn1_w_gate[0], ffn1_w_up[0], ffn1_w_down[0], row(ffn1_post_norm[0]))
    ffn2 = (row(ffn2_pre_norm[0]), ffn2_w_gate[0], ffn2_w_up[0], ffn2_w_down[0], row(ffn2_post_norm[0]))
    proj_w = (row(mix_pre_norm[0]), w, wga, wa2, ba, wswa)

    hm = _ffn_call(meta_tokens.astype(F32), *ffn1, tm=N_META)
    cos_m, sin_m = _rope_tables(jnp.arange(N_META))
    _, gk_m, gv_m, _, la_m, _, sk_m, sv_m = _proj_call(hm, *proj_w, cos_m, sin_m, tm=N_META, seq_rows=N_META)
    front = GLA_CHUNK - N_META
    pad_front = lambda a: jnp.pad(a, ((front, 0), (0, 0)))

    xs = x.reshape(batch * seq, D_MODEL)
    h1 = _ffn_call(xs, *ffn1, tm=tm)
    cos_s, sin_s = _rope_tables(N_META + jnp.arange(seq))
    gq, gk, gv, gg, la, sq, sk, sv = _proj_call(h1, *proj_w, cos_s, sin_s, tm=tm, seq_rows=seq)
    o_gla = _gla_call(gq, gk, gv, la, pad_front(gk_m), pad_front(gv_m), pad_front(la_m),
                      batch=batch, seq_rows=seq)
    kmbd, vmseg = _swa_meta_operands(sk_m, sv_m)
    o_swa = _swa_call(sq, sk, sv, kmbd, vmseg, swa_sinks.reshape(1, SWA_Q_HEADS).astype(F32),
                      batch=batch, seq_rows=seq)
    h2 = _out_call(o_gla, gg, o_swa, h1, row(gla_out_norm[0]), row(swa_out_norm[0]),
                   w_out[0], row(mix_post_norm[0]), tm=tm)
    out = _ffn_call(h2, *ffn2, tm=tm)
    return out.reshape(batch, seq, D_MODEL)
```

```python
import functools

import numpy as np
import jax
import jax.numpy as jnp
from jax import lax
from jax.experimental import pallas as pl
from jax.experimental.pallas import tpu as pltpu

F32 = jnp.float32
BF16 = jnp.bfloat16

D_MODEL = 1024
D_FF = 2816
N_META = 16
NORM_EPS = 1e-6
NEG_INF = -1e30
ROPE_THETA = 10000.0

GLA_HEADS = 4
GLA_DK = 64
GLA_DV = 128
GLA_KW = GLA_HEADS * GLA_DK
GLA_VW = GLA_HEADS * GLA_DV
GLA_TAU = 16.0
GLA_CHUNK = 64
GLA_SUB = 8
GLA_ROWS = 512
LOG2E = 1.4426950408889634

SWA_HEAD_DIM = 64
SWA_Q_HEADS = 8
SWA_KV_HEADS = 2
SWA_QW = SWA_Q_HEADS * SWA_HEAD_DIM
SWA_KW = SWA_KV_HEADS * SWA_HEAD_DIM
SWA_BLOCK = 128
SWA_QBLOCKS = 4

LANES = 128
GA_PAD = LANES
FF_CHUNK = 256

VMEM_LIMIT = 56 * 1024 * 1024


def _dot(a, b):
    return jnp.dot(a, b, preferred_element_type=F32)


def _dot_nt(a, b):
    return lax.dot_general(a, b, (((1,), (1,)), ((), ())), preferred_element_type=F32)


def _dot_tn(a, b):
    return lax.dot_general(a, b, (((0,), (0,)), ((), ())), preferred_element_type=F32)


def _rms(x, w):
    ms = jnp.mean(x * x, axis=-1, keepdims=True)
    return x * lax.rsqrt(ms + NORM_EPS) * w


def _const_spec(shape):
    return pl.BlockSpec(shape, lambda *_: (0,) * len(shape), pipeline_mode=pl.Buffered(1))


def _ffn_rows(x, pre_ref, wg_ref, wu_ref, wd_ref, post_ref, act_ref):
    xn = _rms(x, pre_ref[...]).astype(wg_ref.dtype)
    for c in range(D_FF // FF_CHUNK):
        cols = slice(c * FF_CHUNK, (c + 1) * FF_CHUNK)
        g = _dot(xn, wg_ref[:, cols])
        u = _dot(xn, wu_ref[:, cols])
        act_ref[:, cols] = (g * jax.nn.sigmoid(g) * u).astype(act_ref.dtype)
    f = _dot(act_ref[...], wd_ref[...])
    return x + 0.5 * _rms(f, post_ref[...])


def _ffn_body(x_ref, pre_ref, wg_ref, wu_ref, wd_ref, post_ref, o_ref, act_ref):
    o_ref[...] = _ffn_rows(x_ref[...], pre_ref, wg_ref, wu_ref, wd_ref, post_ref, act_ref)


def _ffn_call(x, pre, wg, wu, wd, post, *, tm):
    rows = x.shape[0]
    return pl.pallas_call(
        _ffn_body,
        grid=(rows // tm,),
        in_specs=[
            pl.BlockSpec((tm, D_MODEL), lambda i: (i, 0)),
            _const_spec((1, D_MODEL)),
            _const_spec((D_MODEL, D_FF)),
            _const_spec((D_MODEL, D_FF)),
            _const_spec((D_FF, D_MODEL)),
            _const_spec((1, D_MODEL)),
        ],
        out_specs=pl.BlockSpec((tm, D_MODEL), lambda i: (i, 0)),
        out_shape=jax.ShapeDtypeStruct((rows, D_MODEL), F32),
        scratch_shapes=[pltpu.VMEM((tm, D_FF), wg.dtype)],
        compiler_params=pltpu.CompilerParams(
            dimension_semantics=("arbitrary",), vmem_limit_bytes=VMEM_LIMIT),
        name=f"ffn_tm{tm}",
    )(x, pre, wg, wu, wd, post)


def _rope(x, cos, sin_signed, first_half):
    out = []
    for g in range(x.shape[1] // LANES):
        xg = x[:, g * LANES:(g + 1) * LANES]
        fwd = pltpu.roll(xg, LANES - 32, axis=1)
        bwd = pltpu.roll(xg, 32, axis=1)
        out.append(xg * cos + jnp.where(first_half, fwd, bwd) * sin_signed)
    return out[0] if len(out) == 1 else jnp.concatenate(out, axis=1)


def _proj_body(h_ref, nw_ref, wgla_ref, wga_ref, wa2_ref, ba_ref, wswa_ref,
               row_cos_ref, row_sin_ref, step_cos_ref, step_sin_ref,
               gq_ref, gk_ref, gv_ref, gg_ref, la_ref, sq_ref, sk_ref, sv_ref, *, sub, seq_blocks):
    tm = h_ref.shape[0]
    lane = lax.broadcasted_iota(jnp.int32, (1, LANES), 1)
    first_half = (lane % SWA_HEAD_DIM) < (SWA_HEAD_DIM // 2)
    step = pl.program_id(0) % seq_blocks
    ca = step_cos_ref[pl.ds(step, 1), :]
    sa = step_sin_ref[pl.ds(step, 1), :]
    for r0 in range(0, tm, sub):
        rows = slice(r0, r0 + sub)
        hn = _rms(h_ref[rows, :], nw_ref[...])
        pg = _dot(hn, wgla_ref[...])
        gq_ref[rows, :] = (pg[:, 0:GLA_KW] * (GLA_DK ** -0.5)).astype(BF16)
        gk_ref[rows, :] = pg[:, GLA_KW:2 * GLA_KW].astype(BF16)
        gv_ref[rows, :] = pg[:, 2 * GLA_KW:2 * GLA_KW + GLA_VW].astype(BF16)
        gg_ref[rows, :] = pg[:, 2 * GLA_KW + GLA_VW:].astype(BF16)

        ga = _dot(hn, wga_ref[...])
        z = _dot(ga, wa2_ref[...]) + ba_ref[...]
        log_sig = jnp.minimum(z, 0.0) - jnp.log1p(jnp.exp(-jnp.abs(z)))
        la_ref[rows, :] = log_sig * (1.0 / GLA_TAU)

        ps = _dot(hn, wswa_ref[...])
        cb = row_cos_ref[rows, :]
        sb = row_sin_ref[rows, :]
        cos = ca * cb - sa * sb
        sin_signed = sa * cb + ca * sb
        sq = _rope(ps[:, 0:SWA_QW], cos, sin_signed, first_half)
        sq_ref[rows, :] = (sq * (SWA_HEAD_DIM ** -0.5)).astype(BF16)
        sk_ref[rows, :] = _rope(ps[:, SWA_QW:SWA_QW + SWA_KW], cos, sin_signed, first_half).astype(BF16)
        sv_ref[rows, :] = ps[:, SWA_QW + SWA_KW:].astype(BF16)


def _rope_tables(first_pos, tm, seq_blocks):
    inv_freq = 1.0 / (ROPE_THETA ** (np.arange(0, SWA_HEAD_DIM, 2, dtype=np.float64) / SWA_HEAD_DIM))

    def tables(pos):
        ang = pos.astype(np.float64)[:, None] * inv_freq[None, :]
        cos, sin = np.cos(ang), np.sin(ang)
        return (jnp.asarray(np.concatenate([cos, cos, cos, cos], axis=-1), F32),
                jnp.asarray(np.concatenate([-sin, sin, -sin, sin], axis=-1), F32))

    return tables(first_pos + np.arange(tm)) + tables(tm * np.arange(seq_blocks))


def _proj_call(h, nw, wgla, wga, wa2, ba, wswa, *, tm, sub, seq_rows, first_pos):
    rows = h.shape[0]
    seq_blocks = seq_rows // tm
    row_spec = lambda w: pl.BlockSpec((tm, w), lambda i: (i, 0))
    tabs = _rope_tables(first_pos, tm, seq_blocks)
    widths = (GLA_KW, GLA_KW, GLA_VW, GLA_VW, GLA_KW, SWA_QW, SWA_KW, SWA_KW)
    dtypes = (BF16, BF16, BF16, BF16, F32, BF16, BF16, BF16)
    return pl.pallas_call(
        functools.partial(_proj_body, sub=sub, seq_blocks=seq_blocks),
        grid=(rows // tm,),
        in_specs=[
            row_spec(D_MODEL),
            _const_spec((1, D_MODEL)),
            _const_spec(wgla.shape),
            _const_spec(wga.shape),
            _const_spec(wa2.shape),
            _const_spec((1, GLA_KW)),
            _const_spec(wswa.shape),
        ] + [_const_spec(t.shape) for t in tabs],
        out_specs=[row_spec(w) for w in widths],
        out_shape=[jax.ShapeDtypeStruct((rows, w), dt) for w, dt in zip(widths, dtypes)],
        compiler_params=pltpu.CompilerParams(
            dimension_semantics=("arbitrary",), vmem_limit_bytes=VMEM_LIMIT),
        name=f"proj_tm{tm}",
    )(h, nw, wgla, wga, wa2, ba, wswa, *tabs)


_FAR_KEY_SLABS = ((0, 32, 31), (0, 8, 7), (0, 16, 15), (0, 24, 23), (32, 40, 39), (32, 48, 47), (32, 56, 55))
_FAR_QUERY_SLABS = ((32, 64), (8, 32), (40, 64))
_FAR_QUERY_REFS = ((31, 32), (7, 8), (15, 8), (23, 8), (39, 8), (47, 8), (55, 8))
_FAR_Q = sum(b - a for a, b in _FAR_QUERY_SLABS)
_FAR_K = sum(b - a for a, b, _ in _FAR_KEY_SLABS)


def _far_mask_np():
    m = np.zeros((_FAR_Q, _FAR_K), np.float32)
    m[0:32, 0:32] = 1.0
    col = 32
    row = 32
    for _half in range(2):
        for grp in range(1, 4):
            m[row:row + 8, col:col + 8 * grp] = 1.0
            row += 8
            col += 8 * grp
    return m


def _near_select_np():
    s = np.zeros((GLA_SUB, GLA_HEADS, GLA_DK, GLA_HEADS, GLA_CHUNK // GLA_SUB, GLA_SUB), np.float32)
    for jl in range(GLA_SUB):
        for h in range(GLA_HEADS):
            s[jl, h, :, h, :, jl] = 1.0
    return s.reshape(GLA_SUB * GLA_KW, GLA_KW)


def _cumsum_chunks(x):
    rows = x.shape[0]
    x3 = x.reshape(rows // GLA_SUB, GLA_SUB, GLA_KW)
    sub = lax.broadcasted_iota(jnp.int32, x3.shape, 1)
    for s in (1, 2, 4):
        x3 = x3 + jnp.where(sub >= s, pltpu.roll(x3, s, axis=1), 0.0)
    x = x3.reshape(rows, GLA_KW)
    out = []
    for c in range(rows // GLA_CHUNK):
        carry = None
        for t in range(GLA_CHUNK // GLA_SUB):
            lo = c * GLA_CHUNK + t * GLA_SUB
            blk = x[lo:lo + GLA_SUB]
            if carry is not None:
                blk = blk + carry
            out.append(blk)
            carry = blk[GLA_SUB - 1:GLA_SUB, :]
    return jnp.concatenate(out, axis=0)


def _state_update(state_t, kc, vc_bf, bc, head_lane):
    b_last = bc[GLA_CHUNK - 1:GLA_CHUNK, :]
    k2 = (kc * jnp.exp2(b_last - bc)).astype(BF16)
    u = _dot_tn(vc_bf, k2)
    upd = jnp.zeros((GLA_DV, GLA_KW), F32)
    for h in range(GLA_HEADS):
        upd = upd + jnp.where(head_lane[h], u[h * GLA_DV:(h + 1) * GLA_DV, :], 0.0)
    return state_t * jnp.exp2(b_last) + upd


def _gla_chunk(qc, kc, vc, bc, near, state_t, head_lane, far_mask, kbd_mask, vbd_mask):
    vc_bf = vc.astype(BF16)

    def ref_rows(r, n):
        return jnp.broadcast_to(bc[r:r + 1, :], (n, GLA_KW))

    qb = (qc * jnp.exp2(bc)).astype(BF16)
    lhs = jnp.concatenate([jnp.where(head_lane[h], qb, 0) for h in range(GLA_HEADS)], axis=0)
    o_inter = _dot_nt(lhs, state_t.astype(BF16))

    q_rows = jnp.concatenate([qc[a:b] for a, b in _FAR_QUERY_SLABS], axis=0)
    bq = jnp.concatenate([bc[a:b] for a, b in _FAR_QUERY_SLABS], axis=0)
    rq = jnp.concatenate([ref_rows(r, n) for r, n in _FAR_QUERY_REFS], axis=0)
    q_far = (q_rows * jnp.exp2(bq - rq)).astype(BF16)
    k_rows = jnp.concatenate([kc[a:b] for a, b, _ in _FAR_KEY_SLABS], axis=0)
    bk = jnp.concatenate([bc[a:b] for a, b, _ in _FAR_KEY_SLABS], axis=0)
    rk = jnp.concatenate([ref_rows(r, b - a) for a, b, r in _FAR_KEY_SLABS], axis=0)
    k_far_t = (k_rows * jnp.exp2(rk - bk)).T.astype(BF16)
    k_bd = jnp.where(kbd_mask, jnp.concatenate([k_far_t] * GLA_HEADS, axis=1), 0)
    p = _dot(q_far, k_bd)
    p = jnp.where(far_mask, p, 0.0).astype(BF16)
    v_far = jnp.concatenate([vc[a:b] for a, b, _ in _FAR_KEY_SLABS], axis=0).astype(BF16)

    v_bd = jnp.where(vbd_mask, jnp.concatenate([vc_bf] * GLA_HEADS, axis=0), 0)
    o_near = _dot(near, v_bd)

    zeros8 = jnp.zeros((8, GLA_DV), F32)
    outs = []
    for h in range(GLA_HEADS):
        cols = slice(h * GLA_DV, (h + 1) * GLA_DV)
        of = _dot(p[:, cols], v_far[:, cols])
        top = jnp.concatenate([zeros8, of[32:56]], axis=0)
        bot = of[0:32] + jnp.concatenate([zeros8, of[56:80]], axis=0)
        outs.append(jnp.concatenate([top, bot], axis=0) + o_inter[h * GLA_CHUNK:(h + 1) * GLA_CHUNK])
    o = jnp.concatenate(outs, axis=1) + o_near
    return o, _state_update(state_t, kc, vc_bf, bc, head_lane)


def _gla_body(q_ref, k_ref, v_ref, la_ref, km_ref, vm_ref, lam_ref, sel_ref, fmask_ref,
              o_ref, state_ref):
    lane = lax.broadcasted_iota(jnp.int32, (1, GLA_KW), 1)
    head_lane = [(lane // GLA_DK) == h for h in range(GLA_HEADS)]

    @pl.when(pl.program_id(1) == 0)
    def _():
        bm = _cumsum_chunks(lam_ref[...] * LOG2E)
        zero = jnp.zeros((GLA_DV, GLA_KW), F32)
        state_ref[...] = _state_update(zero, km_ref[...].astype(F32), vm_ref[...], bm, head_lane)

    q = q_ref[...].astype(F32)
    k = k_ref[...].astype(F32)
    v = v_ref[...].astype(F32)
    b = _cumsum_chunks(la_ref[...] * LOG2E)

    nblk = GLA_ROWS // GLA_SUB
    q3 = q.reshape(nblk, GLA_SUB, GLA_KW)
    k3 = k.reshape(nblk, GLA_SUB, GLA_KW)
    b3 = b.reshape(nblk, GLA_SUB, GLA_KW)
    terms = []
    for jl in range(GLA_SUB):
        decay = jnp.exp2(jnp.minimum(b3 - b3[:, jl:jl + 1, :], 0.0))
        t = q3 * k3[:, jl:jl + 1, :] * decay
        terms.append(t.reshape(GLA_ROWS, GLA_KW).astype(BF16))
    near = _dot(jnp.concatenate(terms, axis=1), sel_ref[...])
    row = lax.broadcasted_iota(jnp.int32, (GLA_ROWS, GLA_KW), 0)
    col = lax.broadcasted_iota(jnp.int32, (GLA_ROWS, GLA_KW), 1)
    blocks = GLA_CHUNK // GLA_SUB
    near_mask = jnp.logical_and((row // GLA_SUB) % blocks == (col // GLA_SUB) % blocks,
                                col % GLA_SUB <= row % GLA_SUB)
    near = jnp.where(near_mask, near, 0.0).astype(BF16)

    far_mask = fmask_ref[...] > 0.5
    krow = lax.broadcasted_iota(jnp.int32, (GLA_KW, GLA_HEADS * _FAR_K), 0)
    kcol = lax.broadcasted_iota(jnp.int32, (GLA_KW, GLA_HEADS * _FAR_K), 1)
    kbd_mask = (krow // GLA_DK) == (kcol // _FAR_K)
    vrow = lax.broadcasted_iota(jnp.int32, (GLA_HEADS * GLA_CHUNK, GLA_VW), 0)
    vcol = lax.broadcasted_iota(jnp.int32, (GLA_HEADS * GLA_CHUNK, GLA_VW), 1)
    vbd_mask = (vrow // GLA_CHUNK) == (vcol // GLA_DV)

    state_t = state_ref[...]
    for c in range(GLA_ROWS // GLA_CHUNK):
        rows = slice(c * GLA_CHUNK, (c + 1) * GLA_CHUNK)
        o, state_t = _gla_chunk(q[rows], k[rows], v[rows], b[rows], near[rows], state_t,
                                head_lane, far_mask, kbd_mask, vbd_mask)
        o_ref[rows, :] = o.astype(BF16)
    state_ref[...] = state_t


def _gla_call(gq, gk, gv, la, km, vm, lam, *, batch, seq_rows):
    nsteps = seq_rows // GLA_ROWS
    row_spec = lambda w: pl.BlockSpec((GLA_ROWS, w), lambda b, i: (b * nsteps + i, 0))
    sel = jnp.asarray(_near_select_np(), BF16)
    fmask = jnp.asarray(np.tile(_far_mask_np(), (1, GLA_HEADS)), F32)
    return pl.pallas_call(
        _gla_body,
        grid=(batch, nsteps),
        in_specs=[
            row_spec(GLA_KW), row_spec(GLA_KW), row_spec(GLA_VW), row_spec(GLA_KW),
            _const_spec((GLA_CHUNK, GLA_KW)), _const_spec((GLA_CHUNK, GLA_VW)), _const_spec((GLA_CHUNK, GLA_KW)),
            _const_spec(sel.shape), _const_spec(fmask.shape),
        ],
        out_specs=row_spec(GLA_VW),
        out_shape=jax.ShapeDtypeStruct((batch * seq_rows, GLA_VW), BF16),
        scratch_shapes=[pltpu.VMEM((GLA_DV, GLA_KW), F32)],
        compiler_params=pltpu.CompilerParams(
            dimension_semantics=("arbitrary", "arbitrary"), vmem_limit_bytes=VMEM_LIMIT),
        name="gla",
    )(gq, gk, gv, la, km, vm, lam, sel, fmask)


def _half_variants(x, low_lane):
    swapped = pltpu.roll(x, SWA_HEAD_DIM, axis=1)
    z = jnp.zeros_like(x)
    return (
        (jnp.where(low_lane, x, z).astype(BF16), jnp.where(low_lane, z, swapped).astype(BF16)),
        (jnp.where(low_lane, swapped, z).astype(BF16), jnp.where(low_lane, z, x).astype(BF16)),
    )


def _swa_body(sink_ref, q_ref, kp_ref, kc_ref, vp_ref, vc_ref, kmbd_ref, vmseg_ref, o_ref):
    first = pl.program_id(1) == 0
    t = SWA_BLOCK
    lane = lax.broadcasted_iota(jnp.int32, (1, LANES), 1)
    low_lane = lane < SWA_HEAD_DIM
    meta_head = lane // N_META
    k_var = _half_variants(jnp.concatenate([kp_ref[...], kc_ref[...]], axis=0).astype(F32), low_lane)
    v_var = _half_variants(jnp.concatenate([vp_ref[...], vc_ref[...]], axis=0).astype(F32), low_lane)
    ones = jnp.ones((2 * t, LANES), BF16)

    r = lax.broadcasted_iota(jnp.int32, (t, t), 0)
    c = lax.broadcasted_iota(jnp.int32, (t, t), 1)
    causal = c <= r
    not_causal = jnp.logical_not(causal)
    not_causal_first = jnp.logical_and(not_causal, jnp.logical_not(first))

    for j in range(SWA_QBLOCKS):
        qj = q_ref[j * t:(j + 1) * t, :]
        keys = slice(j * t, (j + 2) * t)
        use_prev = not_causal_first if j == 0 else not_causal
        sm = _dot(qj, kmbd_ref[...])
        pv = [None] * SWA_Q_HEADS
        m_head = [None] * SWA_Q_HEADS
        for g in range(SWA_KV_HEADS):
            lhs = jnp.concatenate([qj[:, 2 * g * LANES:(2 * g + 1) * LANES],
                                   qj[:, (2 * g + 1) * LANES:(2 * g + 2) * LANES]], axis=0)
            for half in range(2):
                s_all = _dot_nt(lhs, k_var[g][half][keys])
                rhs = jnp.concatenate([v_var[g][half][keys], ones], axis=1)
                for grp in range(2):
                    head = 4 * g + 2 * grp + half
                    s = s_all[grp * t:(grp + 1) * t]
                    s_sel = jnp.where(causal, s[:, t:2 * t], jnp.where(use_prev, s[:, 0:t], NEG_INF))
                    sm_h = jnp.where(meta_head == head, sm, NEG_INF)
                    m = jnp.max(jnp.maximum(s_sel, sm_h), axis=-1, keepdims=True)
                    m = jnp.maximum(m, sink_ref[0, head])
                    p = jnp.exp(s_sel - m)
                    p_band = jnp.concatenate([jnp.where(causal, 0.0, p), jnp.where(causal, p, 0.0)], axis=1)
                    pv[head] = _dot(p_band.astype(BF16), rhs)
                    m_head[head] = m
        m_meta = jnp.zeros((t, LANES), F32)
        for head in range(SWA_Q_HEADS):
            m_meta = jnp.where(meta_head == head, m_head[head], m_meta)
        pm = jnp.exp(sm - m_meta).astype(BF16)
        mo = _dot(pm, vmseg_ref[...])
        for grp in range(SWA_Q_HEADS // 2):
            he, ho = 2 * grp, 2 * grp + 1
            cols = slice(grp * LANES, (grp + 1) * LANES)
            num = pv[he][:, 0:LANES] + pv[ho][:, 0:LANES] + mo[:, cols]
            sink_term = jnp.where(low_lane, jnp.exp(sink_ref[0, he] - m_head[he]),
                                  jnp.exp(sink_ref[0, ho] - m_head[ho]))
            den = (jnp.where(low_lane, pv[he][:, LANES:], pv[ho][:, LANES:])
                   + mo[:, SWA_QW + grp * LANES:SWA_QW + (grp + 1) * LANES] + sink_term)
            o_ref[j * t:(j + 1) * t, cols] = (num / den).astype(BF16)


def _swa_meta_operands(sk_m, sv_m):
    eye = jnp.eye(SWA_Q_HEADS, dtype=F32)
    per_head = lambda a: jnp.stack(
        [a[:, (h // 4) * SWA_HEAD_DIM:(h // 4 + 1) * SWA_HEAD_DIM] for h in range(SWA_Q_HEADS)]).astype(F32)
    km = per_head(sk_m)
    vm = per_head(sv_m)
    kmbd = jnp.einsum('hmd,hk->hdkm', km, eye).reshape(SWA_QW, SWA_Q_HEADS * N_META)
    vmbd = jnp.einsum('hmd,hk->hmkd', vm, eye).reshape(SWA_Q_HEADS * N_META, SWA_QW)
    seg = jnp.kron(eye, jnp.ones((N_META, SWA_HEAD_DIM), F32))
    return kmbd.astype(BF16), jnp.concatenate([vmbd, seg], axis=1).astype(BF16)


def _swa_call(sq, sk, sv, kmbd, vmseg, sinks, *, batch, seq_rows):
    nb = seq_rows // SWA_BLOCK
    step = SWA_QBLOCKS * SWA_BLOCK
    nsteps = seq_rows // step
    cur = lambda w: pl.BlockSpec((step, w), lambda b, i: (b * nsteps + i, 0))
    prev = lambda w: pl.BlockSpec(
        (SWA_BLOCK, w), lambda b, i: (b * nb + jnp.maximum(i * SWA_QBLOCKS - 1, 0), 0))
    return pl.pallas_call(
        _swa_body,
        grid=(batch, nsteps),
        in_specs=[
            pl.BlockSpec(memory_space=pltpu.SMEM),
            cur(SWA_QW), prev(SWA_KW), cur(SWA_KW), prev(SWA_KW), cur(SWA_KW),
            _const_spec(kmbd.shape), _const_spec(vmseg.shape),
        ],
        out_specs=cur(SWA_QW),
        out_shape=jax.ShapeDtypeStruct((batch * seq_rows, SWA_QW), BF16),
        compiler_params=pltpu.CompilerParams(
            dimension_semantics=("arbitrary", "arbitrary"), vmem_limit_bytes=VMEM_LIMIT),
        name="swa",
    )(sinks, sq, sk, sk, sv, sv, kmbd, vmseg)


def _mix_ffn_body(og_ref, gg_ref, os_ref, h_ref, gnw_ref, snw_ref, wout_ref, pnw_ref,
                  pre_ref, wg_ref, wu_ref, wd_ref, post_ref, o_ref, act_ref):
    og = og_ref[...].astype(F32)
    gg = gg_ref[...].astype(F32)
    gnw = gnw_ref[...]
    parts = []
    for h in range(GLA_HEADS):
        cols = slice(h * GLA_DV, (h + 1) * GLA_DV)
        gate = gg[:, cols]
        parts.append(_rms(og[:, cols], gnw) * (gate * jax.nn.sigmoid(gate)))
    parts.append(_rms(os_ref[...].astype(F32), snw_ref[...]))
    mixed = jnp.concatenate(parts, axis=1).astype(wout_ref.dtype)
    m = _dot(mixed, wout_ref[...])
    h2 = h_ref[...] + _rms(m, pnw_ref[...])
    o_ref[...] = _ffn_rows(h2, pre_ref, wg_ref, wu_ref, wd_ref, post_ref, act_ref)


def _mix_ffn_call(og, gg, osw, h, gnw, snw, wout, pnw, pre, wg, wu, wd, post, *, tm):
    rows = h.shape[0]
    row_spec = lambda w: pl.BlockSpec((tm, w), lambda i: (i, 0))
    return pl.pallas_call(
        _mix_ffn_body,
        grid=(rows // tm,),
        in_specs=[
            row_spec(GLA_VW), row_spec(GLA_VW), row_spec(SWA_QW), row_spec(D_MODEL),
            _const_spec((1, GLA_DV)), _const_spec((1, SWA_QW)), _const_spec((D_MODEL, D_MODEL)),
            _const_spec((1, D_MODEL)),
            _const_spec((1, D_MODEL)),
            _const_spec((D_MODEL, D_FF)),
            _const_spec((D_MODEL, D_FF)),
            _const_spec((D_FF, D_MODEL)),
            _const_spec((1, D_MODEL)),
        ],
        out_specs=row_spec(D_MODEL),
        out_shape=jax.ShapeDtypeStruct((rows, D_MODEL), F32),
        scratch_shapes=[pltpu.VMEM((tm, D_FF), wg.dtype)],
        compiler_params=pltpu.CompilerParams(
            dimension_semantics=("arbitrary",), vmem_limit_bytes=VMEM_LIMIT),
        name="mix_ffn",
    )(og, gg, osw, h, gnw, snw, wout, pnw, pre, wg, wu, wd, post)


def kernel(x, meta_tokens, ffn1_pre_norm, ffn1_w_gate, ffn1_w_up, ffn1_w_down, ffn1_post_norm, mix_pre_norm, w_in, gla_w_a2, gla_b_a, gla_out_norm, swa_sinks, swa_out_norm, w_out, mix_post_norm, ffn2_pre_norm, ffn2_w_gate, ffn2_w_up, ffn2_w_down, ffn2_post_norm):
    batch, seq, _ = x.shape
    depth = ffn1_pre_norm.shape[0]
    assert depth == 1 and seq % GLA_ROWS == 0
    row = lambda a: a.reshape(1, -1).astype(F32)
    tm = 512

    w = w_in[0]
    n_gla = 2 * GLA_KW + 2 * GLA_VW
    wgla = w[:, :n_gla]
    wga = jnp.pad(w[:, n_gla:n_gla + N_META], ((0, 0), (0, GA_PAD - N_META)))
    wswa = w[:, n_gla + N_META:]
    wa2 = jnp.pad(gla_w_a2[0], ((0, GA_PAD - N_META), (0, 0)))
    ba = row(gla_b_a[0])

    ffn1 = (row(ffn1_pre_norm[0]), ffn1_w_gate[0], ffn1_w_up[0], ffn1_w_down[0], row(ffn1_post_norm[0]))
    ffn2 = (row(ffn2_pre_norm[0]), ffn2_w_gate[0].astype(BF16), ffn2_w_up[0].astype(BF16),
            ffn2_w_down[0].astype(BF16), row(ffn2_post_norm[0]))
    proj_w = (row(mix_pre_norm[0]), wgla, wga, wa2, ba, wswa)

    hm = _ffn_call(meta_tokens.astype(F32), *ffn1, tm=N_META)
    _, gk_m, gv_m, _, la_m, _, sk_m, sv_m = _proj_call(
        hm, *proj_w, tm=N_META, sub=N_META, seq_rows=N_META, first_pos=0)
    front = GLA_CHUNK - N_META
    pad_front = lambda a: jnp.pad(a, ((front, 0), (0, 0)))

    xs = x.reshape(batch * seq, D_MODEL)
    h1 = _ffn_call(xs, *ffn1, tm=tm)
    gq, gk, gv, gg, la, sq, sk, sv = _proj_call(
        h1, *proj_w, tm=tm, sub=tm // 2, seq_rows=seq, first_pos=N_META)
    o_gla = _gla_call(gq, gk, gv, la, pad_front(gk_m), pad_front(gv_m), pad_front(la_m),
                      batch=batch, seq_rows=seq)
    kmbd, vmseg = _swa_meta_operands(sk_m, sv_m)
    o_swa = _swa_call(sq, sk, sv, kmbd, vmseg, swa_sinks.reshape(1, SWA_Q_HEADS).astype(F32),
                      batch=batch, seq_rows=seq)
    out = _mix_ffn_call(o_gla, gg, o_swa, h1, row(gla_out_norm[0]), row(swa_out_norm[0]),
                        w_out[0].astype(BF16), row(mix_post_norm[0]), *ffn2, tm=tm)
    return out.reshape(batch, seq, D_MODEL)
```

```python
import functools

import numpy as np
import jax
import jax.numpy as jnp
from jax import lax
from jax.experimental import pallas as pl
from jax.experimental.pallas import tpu as pltpu

F32 = jnp.float32
BF16 = jnp.bfloat16

D_MODEL = 1024
D_FF = 2816
N_META = 16
NORM_EPS = 1e-6
NEG_INF = -1e30
ROPE_THETA = 10000.0

GLA_HEADS = 4
GLA_DK = 64
GLA_DV = 128
GLA_KW = GLA_HEADS * GLA_DK
GLA_VW = GLA_HEADS * GLA_DV
GLA_TAU = 16.0
GLA_CHUNK = 64
GLA_SUB = 8
GLA_ROWS = 512
LOG2E = 1.4426950408889634

SWA_HEAD_DIM = 64
SWA_Q_HEADS = 8
SWA_KV_HEADS = 2
SWA_QW = SWA_Q_HEADS * SWA_HEAD_DIM
SWA_KW = SWA_KV_HEADS * SWA_HEAD_DIM
SWA_BLOCK = 128
SWA_QBLOCKS = 4

LANES = 128
GA_PAD = LANES
FF_CHUNK = 256

VMEM_LIMIT = 56 * 1024 * 1024


def _dot(a, b):
    return jnp.dot(a, b, preferred_element_type=F32)


def _dot_nt(a, b):
    return lax.dot_general(a, b, (((1,), (1,)), ((), ())), preferred_element_type=F32)


def _dot_tn(a, b):
    return lax.dot_general(a, b, (((0,), (0,)), ((), ())), preferred_element_type=F32)


def _rms(x, w):
    ms = jnp.mean(x * x, axis=-1, keepdims=True)
    return x * lax.rsqrt(ms + NORM_EPS) * w


def _const_spec(shape):
    return pl.BlockSpec(shape, lambda *_: (0,) * len(shape), pipeline_mode=pl.Buffered(1))


def _ffn_rows(x, pre_ref, wg_ref, wu_ref, wd_ref, post_ref, act_ref):
    xn = _rms(x, pre_ref[...]).astype(wg_ref.dtype)
    for c in range(D_FF // FF_CHUNK):
        cols = slice(c * FF_CHUNK, (c + 1) * FF_CHUNK)
        g = _dot(xn, wg_ref[:, cols])
        u = _dot(xn, wu_ref[:, cols])
        act_ref[:, cols] = (g * jax.nn.sigmoid(g) * u).astype(act_ref.dtype)
    f = _dot(act_ref[...], wd_ref[...])
    return x + 0.5 * _rms(f, post_ref[...])


def _ffn_body(x_ref, pre_ref, wg_ref, wu_ref, wd_ref, post_ref, o_ref, act_ref):
    o_ref[...] = _ffn_rows(x_ref[...], pre_ref, wg_ref, wu_ref, wd_ref, post_ref, act_ref)


def _ffn_call(x, pre, wg, wu, wd, post, *, tm):
    rows = x.shape[0]
    return pl.pallas_call(
        _ffn_body,
        grid=(rows // tm,),
        in_specs=[
            pl.BlockSpec((tm, D_MODEL), lambda i: (i, 0)),
            _const_spec((1, D_MODEL)),
            _const_spec((D_MODEL, D_FF)),
            _const_spec((D_MODEL, D_FF)),
            _const_spec((D_FF, D_MODEL)),
            _const_spec((1, D_MODEL)),
        ],
        out_specs=pl.BlockSpec((tm, D_MODEL), lambda i: (i, 0)),
        out_shape=jax.ShapeDtypeStruct((rows, D_MODEL), F32),
        scratch_shapes=[pltpu.VMEM((tm, D_FF), wg.dtype)],
        compiler_params=pltpu.CompilerParams(
            dimension_semantics=("arbitrary",), vmem_limit_bytes=VMEM_LIMIT),
        name=f"ffn_tm{tm}",
    )(x, pre, wg, wu, wd, post)


def _rope(x, cos, sin_signed, first_half):
    out = []
    for g in range(x.shape[1] // LANES):
        xg = x[:, g * LANES:(g + 1) * LANES]
        fwd = pltpu.roll(xg, LANES - 32, axis=1)
        bwd = pltpu.roll(xg, 32, axis=1)
        out.append(xg * cos + jnp.where(first_half, fwd, bwd) * sin_signed)
    return out[0] if len(out) == 1 else jnp.concatenate(out, axis=1)


def _proj_body(h_ref, nw_ref, wgla_ref, wga_ref, wa2_ref, ba_ref, wswa_ref,
               row_cos_ref, row_sin_ref, step_cos_ref, step_sin_ref,
               gq_ref, gk_ref, gv_ref, gg_ref, la_ref, sq_ref, sk_ref, sv_ref, *, sub, seq_blocks):
    tm = h_ref.shape[0]
    lane = lax.broadcasted_iota(jnp.int32, (1, LANES), 1)
    first_half = (lane % SWA_HEAD_DIM) < (SWA_HEAD_DIM // 2)
    step = pl.program_id(0) % seq_blocks
    ca = step_cos_ref[pl.ds(step, 1), :]
    sa = step_sin_ref[pl.ds(step, 1), :]
    for r0 in range(0, tm, sub):
        rows = slice(r0, r0 + sub)
        hn = _rms(h_ref[rows, :], nw_ref[...])
        pg = _dot(hn, wgla_ref[...])
        gq_ref[rows, :] = (pg[:, 0:GLA_KW] * (GLA_DK ** -0.5)).astype(BF16)
        gk_ref[rows, :] = pg[:, GLA_KW:2 * GLA_KW].astype(BF16)
        gv_ref[rows, :] = pg[:, 2 * GLA_KW:2 * GLA_KW + GLA_VW].astype(BF16)
        gg_ref[rows, :] = pg[:, 2 * GLA_KW + GLA_VW:].astype(BF16)

        ga = _dot(hn, wga_ref[...])
        z = _dot(ga, wa2_ref[...]) + ba_ref[...]
        log_sig = jnp.minimum(z, 0.0) - jnp.log1p(jnp.exp(-jnp.abs(z)))
        la_ref[rows, :] = log_sig * (1.0 / GLA_TAU)

        ps = _dot(hn, wswa_ref[...])
        cb = row_cos_ref[rows, :]
        sb = row_sin_ref[rows, :]
        cos = ca * cb - sa * sb
        sin_signed = sa * cb + ca * sb
        sq = _rope(ps[:, 0:SWA_QW], cos, sin_signed, first_half)
        sq_ref[rows, :] = (sq * (SWA_HEAD_DIM ** -0.5)).astype(BF16)
        sk_ref[rows, :] = _rope(ps[:, SWA_QW:SWA_QW + SWA_KW], cos, sin_signed, first_half).astype(BF16)
        sv_ref[rows, :] = ps[:, SWA_QW + SWA_KW:].astype(BF16)


def _rope_tables(first_pos, tm, seq_blocks):
    inv_freq = 1.0 / (ROPE_THETA ** (np.arange(0, SWA_HEAD_DIM, 2, dtype=np.float64) / SWA_HEAD_DIM))

    def tables(pos):
        ang = pos.astype(np.float64)[:, None] * inv_freq[None, :]
        cos, sin = np.cos(ang), np.sin(ang)
        return (jnp.asarray(np.concatenate([cos, cos, cos, cos], axis=-1), F32),
                jnp.asarray(np.concatenate([-sin, sin, -sin, sin], axis=-1), F32))

    return tables(first_pos + np.arange(tm)) + tables(tm * np.arange(seq_blocks))


def _proj_call(h, nw, wgla, wga, wa2, ba, wswa, *, tm, sub, seq_rows, first_pos):
    rows = h.shape[0]
    seq_blocks = seq_rows // tm
    row_spec = lambda w: pl.BlockSpec((tm, w), lambda i: (i, 0))
    tabs = _rope_tables(first_pos, tm, seq_blocks)
    widths = (GLA_KW, GLA_KW, GLA_VW, GLA_VW, GLA_KW, SWA_QW, SWA_KW, SWA_KW)
    dtypes = (BF16, BF16, BF16, BF16, F32, BF16, BF16, BF16)
    return pl.pallas_call(
        functools.partial(_proj_body, sub=sub, seq_blocks=seq_blocks),
        grid=(rows // tm,),
        in_specs=[
            row_spec(D_MODEL),
            _const_spec((1, D_MODEL)),
            _const_spec(wgla.shape),
            _const_spec(wga.shape),
            _const_spec(wa2.shape),
            _const_spec((1, GLA_KW)),
            _const_spec(wswa.shape),
        ] + [_const_spec(t.shape) for t in tabs],
        out_specs=[row_spec(w) for w in widths],
        out_shape=[jax.ShapeDtypeStruct((rows, w), dt) for w, dt in zip(widths, dtypes)],
        compiler_params=pltpu.CompilerParams(
            dimension_semantics=("arbitrary",), vmem_limit_bytes=VMEM_LIMIT),
        name=f"proj_tm{tm}",
    )(h, nw, wgla, wga, wa2, ba, wswa, *tabs)


_FAR_KEY_SLABS = ((0, 32, 31), (0, 8, 7), (0, 16, 15), (0, 24, 23), (32, 40, 39), (32, 48, 47), (32, 56, 55))
_FAR_QUERY_SLABS = ((32, 64), (8, 32), (40, 64))
_FAR_QUERY_REFS = ((31, 32), (7, 8), (15, 8), (23, 8), (39, 8), (47, 8), (55, 8))
_FAR_Q = sum(b - a for a, b in _FAR_QUERY_SLABS)
_FAR_K = sum(b - a for a, b, _ in _FAR_KEY_SLABS)


def _far_mask_np():
    m = np.zeros((_FAR_Q, _FAR_K), np.float32)
    m[0:32, 0:32] = 1.0
    col = 32
    row = 32
    for _half in range(2):
        for grp in range(1, 4):
            m[row:row + 8, col:col + 8 * grp] = 1.0
            row += 8
            col += 8 * grp
    return m


def _near_select_np():
    s = np.zeros((GLA_SUB, GLA_HEADS, GLA_DK, GLA_HEADS, GLA_CHUNK // GLA_SUB, GLA_SUB), np.float32)
    for jl in range(GLA_SUB):
        for h in range(GLA_HEADS):
            s[jl, h, :, h, :, jl] = 1.0
    return s.reshape(GLA_SUB * GLA_KW, GLA_KW)


def _cumsum_chunks(x):
    rows = x.shape[0]
    x3 = x.reshape(rows // GLA_SUB, GLA_SUB, GLA_KW)
    sub = lax.broadcasted_iota(jnp.int32, x3.shape, 1)
    for s in (1, 2, 4):
        x3 = x3 + jnp.where(sub >= s, pltpu.roll(x3, s, axis=1), 0.0)
    x = x3.reshape(rows, GLA_KW)
    out = []
    for c in range(rows // GLA_CHUNK):
        carry = None
        for t in range(GLA_CHUNK // GLA_SUB):
            lo = c * GLA_CHUNK + t * GLA_SUB
            blk = x[lo:lo + GLA_SUB]
            if carry is not None:
                blk = blk + carry
            out.append(blk)
            carry = blk[GLA_SUB - 1:GLA_SUB, :]
    return jnp.concatenate(out, axis=0)


def _state_update(state_t, kc, vc_bf, bc, head_lane):
    b_last = bc[GLA_CHUNK - 1:GLA_CHUNK, :]
    k2 = (kc * jnp.exp2(b_last - bc)).astype(BF16)
    upd = None
    for h in range(GLA_HEADS):
        u = _dot_tn(vc_bf[:, h * GLA_DV:(h + 1) * GLA_DV], jnp.where(head_lane[h], k2, 0))
        upd = u if upd is None else upd + u
    return state_t * jnp.exp2(b_last) + upd


def _gla_chunk(qc, kc, vc, bc, near, state_t, head_lane, far_mask, kbd_mask, vbd_mask):
    vc_bf = vc.astype(BF16)

    def ref_rows(r, n):
        return jnp.broadcast_to(bc[r:r + 1, :], (n, GLA_KW))

    qb = (qc * jnp.exp2(bc)).astype(BF16)
    lhs = jnp.concatenate([jnp.where(head_lane[h], qb, 0) for h in range(GLA_HEADS)], axis=0)
    o_inter = _dot_nt(lhs, state_t.astype(BF16))

    q_rows = jnp.concatenate([qc[a:b] for a, b in _FAR_QUERY_SLABS], axis=0)
    bq = jnp.concatenate([bc[a:b] for a, b in _FAR_QUERY_SLABS], axis=0)
    rq = jnp.concatenate([ref_rows(r, n) for r, n in _FAR_QUERY_REFS], axis=0)
    q_far = (q_rows * jnp.exp2(bq - rq)).astype(BF16)
    k_rows = jnp.concatenate([kc[a:b] for a, b, _ in _FAR_KEY_SLABS], axis=0)
    bk = jnp.concatenate([bc[a:b] for a, b, _ in _FAR_KEY_SLABS], axis=0)
    rk = jnp.concatenate([ref_rows(r, b - a) for a, b, r in _FAR_KEY_SLABS], axis=0)
    k_far_t = (k_rows * jnp.exp2(rk - bk)).T.astype(BF16)
    k_bd = jnp.where(kbd_mask, jnp.concatenate([k_far_t] * GLA_HEADS, axis=1), 0)
    p = _dot(q_far, k_bd)
    p = jnp.where(far_mask, p, 0.0).astype(BF16)
    v_far = jnp.concatenate([vc[a:b] for a, b, _ in _FAR_KEY_SLABS], axis=0).astype(BF16)

    v_bd = jnp.where(vbd_mask, jnp.concatenate([vc_bf] * GLA_HEADS, axis=0), 0)
    o_near = _dot(near, v_bd)

    zeros8 = jnp.zeros((8, GLA_DV), F32)
    outs = []
    for h in range(GLA_HEADS):
        cols = slice(h * GLA_DV, (h + 1) * GLA_DV)
        of = _dot(p[:, cols], v_far[:, cols])
        top = jnp.concatenate([zeros8, of[32:56]], axis=0)
        bot = of[0:32] + jnp.concatenate([zeros8, of[56:80]], axis=0)
        outs.append(jnp.concatenate([top, bot], axis=0) + o_inter[h * GLA_CHUNK:(h + 1) * GLA_CHUNK])
    o = jnp.concatenate(outs, axis=1) + o_near
    return o, _state_update(state_t, kc, vc_bf, bc, head_lane)


def _gla_prepare(q_ref, k_ref, la_ref, sel_ref, b_ref, near_ref):
    q = q_ref[...].astype(F32)
    k = k_ref[...].astype(F32)
    b = _cumsum_chunks(la_ref[...] * LOG2E)
    nblk = GLA_ROWS // GLA_SUB
    q3 = q.reshape(nblk, GLA_SUB, GLA_KW)
    k3 = k.reshape(nblk, GLA_SUB, GLA_KW)
    b3 = b.reshape(nblk, GLA_SUB, GLA_KW)
    terms = []
    for jl in range(GLA_SUB):
        decay = jnp.exp2(jnp.minimum(b3 - b3[:, jl:jl + 1, :], 0.0))
        t = q3 * k3[:, jl:jl + 1, :] * decay
        terms.append(t.reshape(GLA_ROWS, GLA_KW).astype(BF16))
    near = _dot(jnp.concatenate(terms, axis=1), sel_ref[...])
    row = lax.broadcasted_iota(jnp.int32, (GLA_ROWS, GLA_KW), 0)
    col = lax.broadcasted_iota(jnp.int32, (GLA_ROWS, GLA_KW), 1)
    blocks = GLA_CHUNK // GLA_SUB
    near_mask = jnp.logical_and((row // GLA_SUB) % blocks == (col // GLA_SUB) % blocks,
                                col % GLA_SUB <= row % GLA_SUB)
    b_ref[...] = b
    near_ref[...] = jnp.where(near_mask, near, 0.0).astype(BF16)


def _gla_body(q_ref, k_ref, v_ref, la_ref, qn_ref, kn_ref, lan_ref, km_ref, vm_ref, lam_ref, sel_ref, fmask_ref,
              o_ref, state_ref, b_ref, near_ref):
    lane = lax.broadcasted_iota(jnp.int32, (1, GLA_KW), 1)
    head_lane = [(lane // GLA_DK) == h for h in range(GLA_HEADS)]

    @pl.when(pl.program_id(1) == 0)
    def _():
        bm = _cumsum_chunks(lam_ref[...] * LOG2E)
        zero = jnp.zeros((GLA_DV, GLA_KW), F32)
        state_ref[...] = _state_update(zero, km_ref[...].astype(F32), vm_ref[...], bm, head_lane)
        _gla_prepare(q_ref, k_ref, la_ref, sel_ref, b_ref, near_ref)

    q = q_ref[...].astype(F32)
    k = k_ref[...].astype(F32)
    v = v_ref[...].astype(F32)
    b = b_ref[...]
    near = near_ref[...]

    far_mask = fmask_ref[...] > 0.5
    krow = lax.broadcasted_iota(jnp.int32, (GLA_KW, GLA_HEADS * _FAR_K), 0)
    kcol = lax.broadcasted_iota(jnp.int32, (GLA_KW, GLA_HEADS * _FAR_K), 1)
    kbd_mask = (krow // GLA_DK) == (kcol // _FAR_K)
    vrow = lax.broadcasted_iota(jnp.int32, (GLA_HEADS * GLA_CHUNK, GLA_VW), 0)
    vcol = lax.broadcasted_iota(jnp.int32, (GLA_HEADS * GLA_CHUNK, GLA_VW), 1)
    vbd_mask = (vrow // GLA_CHUNK) == (vcol // GLA_DV)

    state_t = state_ref[...]
    for c in range(GLA_ROWS // GLA_CHUNK):
        rows = slice(c * GLA_CHUNK, (c + 1) * GLA_CHUNK)
        o, state_t = _gla_chunk(q[rows], k[rows], v[rows], b[rows], near[rows], state_t,
                                head_lane, far_mask, kbd_mask, vbd_mask)
        o_ref[rows, :] = o.astype(BF16)
    state_ref[...] = state_t
    _gla_prepare(qn_ref, kn_ref, lan_ref, sel_ref, b_ref, near_ref)


def _gla_call(gq, gk, gv, la, km, vm, lam, *, batch, seq_rows):
    nsteps = seq_rows // GLA_ROWS
    row_spec = lambda w: pl.BlockSpec((GLA_ROWS, w), lambda b, i: (b * nsteps + i, 0))
    next_spec = lambda w: pl.BlockSpec(
        (GLA_ROWS, w), lambda b, i: (b * nsteps + jnp.minimum(i + 1, nsteps - 1), 0))
    sel = jnp.asarray(_near_select_np(), BF16)
    fmask = jnp.asarray(np.tile(_far_mask_np(), (1, GLA_HEADS)), F32)
    return pl.pallas_call(
        _gla_body,
        grid=(batch, nsteps),
        in_specs=[
            row_spec(GLA_KW), row_spec(GLA_KW), row_spec(GLA_VW), row_spec(GLA_KW),
            next_spec(GLA_KW), next_spec(GLA_KW), next_spec(GLA_KW),
            _const_spec((GLA_CHUNK, GLA_KW)), _const_spec((GLA_CHUNK, GLA_VW)), _const_spec((GLA_CHUNK, GLA_KW)),
            _const_spec(sel.shape), _const_spec(fmask.shape),
        ],
        out_specs=row_spec(GLA_VW),
        out_shape=jax.ShapeDtypeStruct((batch * seq_rows, GLA_VW), BF16),
        scratch_shapes=[pltpu.VMEM((GLA_DV, GLA_KW), F32),
                        pltpu.VMEM((GLA_ROWS, GLA_KW), F32),
                        pltpu.VMEM((GLA_ROWS, GLA_KW), BF16)],
        compiler_params=pltpu.CompilerParams(
            dimension_semantics=("arbitrary", "arbitrary"), vmem_limit_bytes=VMEM_LIMIT),
        name="gla",
    )(gq, gk, gv, la, gq, gk, la, km, vm, lam, sel, fmask)


def _half_variants(x, low_lane):
    swapped = pltpu.roll(x, SWA_HEAD_DIM, axis=1)
    z = jnp.zeros_like(x)
    return (
        (jnp.where(low_lane, x, z).astype(BF16), jnp.where(low_lane, z, swapped).astype(BF16)),
        (jnp.where(low_lane, swapped, z).astype(BF16), jnp.where(low_lane, z, x).astype(BF16)),
    )


def _swa_body(sink_ref, q_ref, kp_ref, kc_ref, vp_ref, vc_ref, kmbd_ref, vmseg_ref, o_ref):
    first = pl.program_id(1) == 0
    t = SWA_BLOCK
    lane = lax.broadcasted_iota(jnp.int32, (1, LANES), 1)
    low_lane = lane < SWA_HEAD_DIM
    meta_head = lane // N_META
    k_var = _half_variants(jnp.concatenate([kp_ref[...], kc_ref[...]], axis=0).astype(F32), low_lane)
    v_var = _half_variants(jnp.concatenate([vp_ref[...], vc_ref[...]], axis=0).astype(F32), low_lane)
    ones = jnp.ones((2 * t, LANES), BF16)

    r = lax.broadcasted_iota(jnp.int32, (t, t), 0)
    c = lax.broadcasted_iota(jnp.int32, (t, t), 1)
    causal = c <= r
    not_causal = jnp.logical_not(causal)
    not_causal_first = jnp.logical_and(not_causal, jnp.logical_not(first))

    for j in range(SWA_QBLOCKS):
        qj = q_ref[j * t:(j + 1) * t, :]
        keys = slice(j * t, (j + 2) * t)
        use_prev = not_causal_first if j == 0 else not_causal
        sm = _dot(qj, kmbd_ref[...])
        pv = [None] * SWA_Q_HEADS
        m_head = [None] * SWA_Q_HEADS
        for g in range(SWA_KV_HEADS):
            lhs = jnp.concatenate([qj[:, 2 * g * LANES:(2 * g + 1) * LANES],
                                   qj[:, (2 * g + 1) * LANES:(2 * g + 2) * LANES]], axis=0)
            for half in range(2):
                s_all = _dot_nt(lhs, k_var[g][half][keys])
                rhs = jnp.concatenate([v_var[g][half][keys], ones], axis=1)
                for grp in range(2):
                    head = 4 * g + 2 * grp + half
                    s = s_all[grp * t:(grp + 1) * t]
                    s_sel = jnp.where(causal, s[:, t:2 * t], jnp.where(use_prev, s[:, 0:t], NEG_INF))
                    sm_h = jnp.where(meta_head == head, sm, NEG_INF)
                    m = jnp.max(jnp.maximum(s_sel, sm_h), axis=-1, keepdims=True)
                    m = jnp.maximum(m, sink_ref[0, head])
                    p = jnp.exp(s_sel - m)
                    p_band = jnp.concatenate([jnp.where(causal, 0.0, p), jnp.where(causal, p, 0.0)], axis=1)
                    pv[head] = _dot(p_band.astype(BF16), rhs)
                    m_head[head] = m
        m_meta = jnp.zeros((t, LANES), F32)
        for head in range(SWA_Q_HEADS):
            m_meta = jnp.where(meta_head == head, m_head[head], m_meta)
        pm = jnp.exp(sm - m_meta).astype(BF16)
        mo = _dot(pm, vmseg_ref[...])
        for grp in range(SWA_Q_HEADS // 2):
            he, ho = 2 * grp, 2 * grp + 1
            cols = slice(grp * LANES, (grp + 1) * LANES)
            num = pv[he][:, 0:LANES] + pv[ho][:, 0:LANES] + mo[:, cols]
            sink_term = jnp.where(low_lane, jnp.exp(sink_ref[0, he] - m_head[he]),
                                  jnp.exp(sink_ref[0, ho] - m_head[ho]))
            den = (jnp.where(low_lane, pv[he][:, LANES:], pv[ho][:, LANES:])
                   + mo[:, SWA_QW + grp * LANES:SWA_QW + (grp + 1) * LANES] + sink_term)
            o_ref[j * t:(j + 1) * t, cols] = (num / den).astype(BF16)


def _swa_meta_operands(sk_m, sv_m):
    eye = jnp.eye(SWA_Q_HEADS, dtype=F32)
    per_head = lambda a: jnp.stack(
        [a[:, (h // 4) * SWA_HEAD_DIM:(h // 4 + 1) * SWA_HEAD_DIM] for h in range(SWA_Q_HEADS)]).astype(F32)
    km = per_head(sk_m)
    vm = per_head(sv_m)
    kmbd = jnp.einsum('hmd,hk->hdkm', km, eye).reshape(SWA_QW, SWA_Q_HEADS * N_META)
    vmbd = jnp.einsum('hmd,hk->hmkd', vm, eye).reshape(SWA_Q_HEADS * N_META, SWA_QW)
    seg = jnp.kron(eye, jnp.ones((N_META, SWA_HEAD_DIM), F32))
    return kmbd.astype(BF16), jnp.concatenate([vmbd, seg], axis=1).astype(BF16)


def _swa_call(sq, sk, sv, kmbd, vmseg, sinks, *, batch, seq_rows):
    nb = seq_rows // SWA_BLOCK
    step = SWA_QBLOCKS * SWA_BLOCK
    nsteps = seq_rows // step
    cur = lambda w: pl.BlockSpec((step, w), lambda b, i: (b * nsteps + i, 0))
    prev = lambda w: pl.BlockSpec(
        (SWA_BLOCK, w), lambda b, i: (b * nb + jnp.maximum(i * SWA_QBLOCKS - 1, 0), 0))
    return pl.pallas_call(
        _swa_body,
        grid=(batch, nsteps),
        in_specs=[
            pl.BlockSpec(memory_space=pltpu.SMEM),
            cur(SWA_QW), prev(SWA_KW), cur(SWA_KW), prev(SWA_KW), cur(SWA_KW),
            _const_spec(kmbd.shape), _const_spec(vmseg.shape),
        ],
        out_specs=cur(SWA_QW),
        out_shape=jax.ShapeDtypeStruct((batch * seq_rows, SWA_QW), BF16),
        compiler_params=pltpu.CompilerParams(
            dimension_semantics=("arbitrary", "arbitrary"), vmem_limit_bytes=VMEM_LIMIT),
        name="swa",
    )(sinks, sq, sk, sk, sv, sv, kmbd, vmseg)


def _mix_ffn_body(og_ref, gg_ref, os_ref, h_ref, gnw_ref, snw_ref, wout_ref, pnw_ref,
                  pre_ref, wg_ref, wu_ref, wd_ref, post_ref, o_ref, act_ref):
    og = og_ref[...].astype(F32)
    gg = gg_ref[...].astype(F32)
    gnw = gnw_ref[...]
    parts = []
    for h in range(GLA_HEADS):
        cols = slice(h * GLA_DV, (h + 1) * GLA_DV)
        gate = gg[:, cols]
        parts.append(_rms(og[:, cols], gnw) * (gate * jax.nn.sigmoid(gate)))
    parts.append(_rms(os_ref[...].astype(F32), snw_ref[...]))
    mixed = jnp.concatenate(parts, axis=1).astype(wout_ref.dtype)
    m = _dot(mixed, wout_ref[...])
    h2 = h_ref[...] + _rms(m, pnw_ref[...])
    o_ref[...] = _ffn_rows(h2, pre_ref, wg_ref, wu_ref, wd_ref, post_ref, act_ref)


def _mix_ffn_call(og, gg, osw, h, gnw, snw, wout, pnw, pre, wg, wu, wd, post, *, tm):
    rows = h.shape[0]
    row_spec = lambda w: pl.BlockSpec((tm, w), lambda i: (i, 0))
    return pl.pallas_call(
        _mix_ffn_body,
        grid=(rows // tm,),
        in_specs=[
            row_spec(GLA_VW), row_spec(GLA_VW), row_spec(SWA_QW), row_spec(D_MODEL),
            _const_spec((1, GLA_DV)), _const_spec((1, SWA_QW)), _const_spec((D_MODEL, D_MODEL)),
            _const_spec((1, D_MODEL)),
            _const_spec((1, D_MODEL)),
            _const_spec((D_MODEL, D_FF)),
            _const_spec((D_MODEL, D_FF)),
            _const_spec((D_FF, D_MODEL)),
            _const_spec((1, D_MODEL)),
        ],
        out_specs=row_spec(D_MODEL),
        out_shape=jax.ShapeDtypeStruct((rows, D_MODEL), F32),
        scratch_shapes=[pltpu.VMEM((tm, D_FF), wg.dtype)],
        compiler_params=pltpu.CompilerParams(
            dimension_semantics=("arbitrary",), vmem_limit_bytes=VMEM_LIMIT),
        name="mix_ffn",
    )(og, gg, osw, h, gnw, snw, wout, pnw, pre, wg, wu, wd, post)


def kernel(x, meta_tokens, ffn1_pre_norm, ffn1_w_gate, ffn1_w_up, ffn1_w_down, ffn1_post_norm, mix_pre_norm, w_in, gla_w_a2, gla_b_a, gla_out_norm, swa_sinks, swa_out_norm, w_out, mix_post_norm, ffn2_pre_norm, ffn2_w_gate, ffn2_w_up, ffn2_w_down, ffn2_post_norm):
    batch, seq, _ = x.shape
    depth = ffn1_pre_norm.shape[0]
    assert depth == 1 and seq % GLA_ROWS == 0
    row = lambda a: a.reshape(1, -1).astype(F32)
    tm = 512

    w = w_in[0]
    n_gla = 2 * GLA_KW + 2 * GLA_VW
    wgla = w[:, :n_gla]
    wga = jnp.pad(w[:, n_gla:n_gla + N_META], ((0, 0), (0, GA_PAD - N_META)))
    wswa = w[:, n_gla + N_META:]
    wa2 = jnp.pad(gla_w_a2[0], ((0, GA_PAD - N_META), (0, 0)))
    ba = row(gla_b_a[0])

    ffn1 = (row(ffn1_pre_norm[0]), ffn1_w_gate[0], ffn1_w_up[0], ffn1_w_down[0], row(ffn1_post_norm[0]))
    ffn2 = (row(ffn2_pre_norm[0]), ffn2_w_gate[0].astype(BF16), ffn2_w_up[0].astype(BF16),
            ffn2_w_down[0].astype(BF16), row(ffn2_post_norm[0]))
    proj_w = (row(mix_pre_norm[0]), wgla, wga, wa2, ba, wswa)

    hm = _ffn_call(meta_tokens.astype(F32), *ffn1, tm=N_META)
    _, gk_m, gv_m, _, la_m, _, sk_m, sv_m = _proj_call(
        hm, *proj_w, tm=N_META, sub=N_META, seq_rows=N_META, first_pos=0)
    front = GLA_CHUNK - N_META
    pad_front = lambda a: jnp.pad(a, ((front, 0), (0, 0)))

    xs = x.reshape(batch * seq, D_MODEL)
    h1 = _ffn_call(xs, *ffn1, tm=tm)
    gq, gk, gv, gg, la, sq, sk, sv = _proj_call(
        h1, *proj_w, tm=tm, sub=tm // 2, seq_rows=seq, first_pos=N_META)
    o_gla = _gla_call(gq, gk, gv, la, pad_front(gk_m), pad_front(gv_m), pad_front(la_m),
                      batch=batch, seq_rows=seq)
    kmbd, vmseg = _swa_meta_operands(sk_m, sv_m)
    o_swa = _swa_call(sq, sk, sv, kmbd, vmseg, swa_sinks.reshape(1, SWA_Q_HEADS).astype(F32),
                      batch=batch, seq_rows=seq)
    out = _mix_ffn_call(o_gla, gg, o_swa, h1, row(gla_out_norm[0]), row(swa_out_norm[0]),
                        w_out[0].astype(BF16), row(mix_post_norm[0]), *ffn2, tm=tm)
    return out.reshape(batch, seq, D_MODEL)
```

```python
import functools
import math

import numpy as np
import jax
import jax.numpy as jnp
from jax import lax
from jax.experimental import pallas as pl
from jax.experimental.pallas import tpu as pltpu

F32 = jnp.float32
BF16 = jnp.bfloat16

D_MODEL = 1024
D_FF = 2816
N_META = 16
NORM_EPS = 1e-6
NEG_INF = -1e30
ROPE_THETA = 10000.0

GLA_HEADS = 4
GLA_DK = 64
GLA_DV = 128
GLA_KW = GLA_HEADS * GLA_DK
GLA_VW = GLA_HEADS * GLA_DV
GLA_TAU = 16.0
GLA_CHUNK = 64
GLA_SUB = 8
GLA_ROWS = 512
LOG2E = 1.4426950408889634

SWA_HEAD_DIM = 64
SWA_Q_HEADS = 8
SWA_KV_HEADS = 2
SWA_QW = SWA_Q_HEADS * SWA_HEAD_DIM
SWA_KW = SWA_KV_HEADS * SWA_HEAD_DIM
SWA_BLOCK = 128
SWA_QBLOCKS = 4
SWA_CAST_SLABS = 16

LANES = 128
FF_CHUNK = 256

VMEM_LIMIT = 56 * 1024 * 1024


def _dot(a, b):
    return jnp.dot(a, b, preferred_element_type=F32)


def _dot_nt(a, b):
    return lax.dot_general(a, b, (((1,), (1,)), ((), ())), preferred_element_type=F32)


def _dot_tn(a, b):
    return lax.dot_general(a, b, (((0,), (0,)), ((), ())), preferred_element_type=F32)


def _rms(x, w):
    ms = jnp.mean(x * x, axis=-1, keepdims=True)
    return x * lax.rsqrt(ms + NORM_EPS) * w


def _const_spec(shape):
    return pl.BlockSpec(shape, lambda *_: (0,) * len(shape), pipeline_mode=pl.Buffered(1))


def _ffn_rows(x, pre_ref, wg_ref, wu_ref, wd_ref, post_ref, act_ref):
    xn = _rms(x, pre_ref[...]).astype(wg_ref.dtype)
    for c in range(D_FF // FF_CHUNK):
        cols = slice(c * FF_CHUNK, (c + 1) * FF_CHUNK)
        g = _dot(xn, wg_ref[:, cols])
        u = _dot(xn, wu_ref[:, cols])
        act_ref[:, cols] = (g * jax.nn.sigmoid(g) * u).astype(act_ref.dtype)
    f = _dot(act_ref[...], wd_ref[...])
    return x + 0.5 * _rms(f, post_ref[...])


def _ffn_body(x_ref, pre_ref, wg_ref, wu_ref, wd_ref, post_ref, o_ref, act_ref):
    o_ref[...] = _ffn_rows(x_ref[...], pre_ref, wg_ref, wu_ref, wd_ref, post_ref, act_ref)


def _ffn_call(x, pre, wg, wu, wd, post, *, tm):
    rows = x.shape[0]
    return pl.pallas_call(
        _ffn_body,
        grid=(rows // tm,),
        in_specs=[
            pl.BlockSpec((tm, D_MODEL), lambda i: (i, 0)),
            _const_spec((1, D_MODEL)),
            _const_spec((D_MODEL, D_FF)),
            _const_spec((D_MODEL, D_FF)),
            _const_spec((D_FF, D_MODEL)),
            _const_spec((1, D_MODEL)),
        ],
        out_specs=pl.BlockSpec((tm, D_MODEL), lambda i: (i, 0)),
        out_shape=jax.ShapeDtypeStruct((rows, D_MODEL), F32),
        scratch_shapes=[pltpu.VMEM((tm, D_FF), wg.dtype)],
        compiler_params=pltpu.CompilerParams(
            dimension_semantics=("arbitrary",), vmem_limit_bytes=VMEM_LIMIT),
        name=f"ffn_tm{tm}",
    )(x, pre, wg, wu, wd, post)


def _rope(x, cos, sin_signed, first_half):
    out = []
    for g in range(x.shape[1] // LANES):
        xg = x[:, g * LANES:(g + 1) * LANES]
        fwd = pltpu.roll(xg, LANES - 32, axis=1)
        bwd = pltpu.roll(xg, 32, axis=1)
        out.append(xg * cos + jnp.where(first_half, fwd, bwd) * sin_signed)
    return out[0] if len(out) == 1 else jnp.concatenate(out, axis=1)


def _proj_body(h_ref, nw_ref, wgla_ref, wga_ref, wa2_ref, ba_ref, wswa_ref,
               row_cos_ref, row_sin_ref, step_cos_ref, step_sin_ref,
               gq_ref, gk_ref, gv_ref, gg_ref, la_ref, sq_ref, sk_ref, sv_ref, wz_ref, *, sub, seq_blocks):
    tm = h_ref.shape[0]

    @pl.when(pl.program_id(0) == 0)
    def _():
        wz_ref[...] = jnp.dot(wga_ref[...], wa2_ref[...], preferred_element_type=F32,
                              precision=lax.Precision.HIGHEST)

    lane = lax.broadcasted_iota(jnp.int32, (1, LANES), 1)
    first_half = (lane % SWA_HEAD_DIM) < (SWA_HEAD_DIM // 2)
    step = pl.program_id(0) % seq_blocks
    ca = step_cos_ref[pl.ds(step, 1), :]
    sa = step_sin_ref[pl.ds(step, 1), :]
    for r0 in range(0, tm, sub):
        rows = slice(r0, r0 + sub)
        hn = _rms(h_ref[rows, :], nw_ref[...])
        pg = _dot(hn, wgla_ref[...])
        gq_ref[rows, :] = (pg[:, 0:GLA_KW] * (GLA_DK ** -0.5)).astype(BF16)
        gk_ref[rows, :] = pg[:, GLA_KW:2 * GLA_KW].astype(BF16)
        gv_ref[rows, :] = pg[:, 2 * GLA_KW:2 * GLA_KW + GLA_VW].astype(BF16)
        gg_ref[rows, :] = pg[:, 2 * GLA_KW + GLA_VW:].astype(BF16)

        z = _dot(hn, wz_ref[...]) + ba_ref[...]
        log_sig = jnp.minimum(z, 0.0) - jnp.log1p(jnp.exp(-jnp.abs(z)))
        la_ref[rows, :] = log_sig * (1.0 / GLA_TAU)

        ps = _dot(hn, wswa_ref[...])
        cb = row_cos_ref[rows, :]
        sb = row_sin_ref[rows, :]
        cos = ca * cb - sa * sb
        sin_signed = sa * cb + ca * sb
        sq = _rope(ps[:, 0:SWA_QW], cos, sin_signed, first_half)
        sq_ref[rows, :] = (sq * (SWA_HEAD_DIM ** -0.5)).astype(BF16)
        sk_ref[rows, :] = _rope(ps[:, SWA_QW:SWA_QW + SWA_KW], cos, sin_signed, first_half).astype(BF16)
        sv_ref[rows, :] = ps[:, SWA_QW + SWA_KW:].astype(BF16)


def _rope_tables(first_pos, tm, seq_blocks):
    inv_freq = 1.0 / (ROPE_THETA ** (np.arange(0, SWA_HEAD_DIM, 2, dtype=np.float64) / SWA_HEAD_DIM))

    def tables(pos):
        ang = pos.astype(np.float64)[:, None] * inv_freq[None, :]
        cos, sin = np.cos(ang), np.sin(ang)
        return (jnp.asarray(np.concatenate([cos, cos, cos, cos], axis=-1), F32),
                jnp.asarray(np.concatenate([-sin, sin, -sin, sin], axis=-1), F32))

    return tables(first_pos + np.arange(tm)) + tables(tm * np.arange(seq_blocks))


def _proj_call(h, nw, wgla, wga, wa2, ba, wswa, *, tm, sub, seq_rows, first_pos):
    rows = h.shape[0]
    seq_blocks = seq_rows // tm
    row_spec = lambda w: pl.BlockSpec((tm, w), lambda i: (i, 0))
    tabs = _rope_tables(first_pos, tm, seq_blocks)
    widths = (GLA_KW, GLA_KW, GLA_VW, GLA_VW, GLA_KW, SWA_QW, SWA_KW, SWA_KW)
    dtypes = (BF16, BF16, BF16, BF16, F32, BF16, BF16, BF16)
    return pl.pallas_call(
        functools.partial(_proj_body, sub=sub, seq_blocks=seq_blocks),
        grid=(rows // tm,),
        in_specs=[
            row_spec(D_MODEL),
            _const_spec((1, D_MODEL)),
            _const_spec(wgla.shape),
            _const_spec(wga.shape),
            _const_spec(wa2.shape),
            _const_spec((1, GLA_KW)),
            _const_spec(wswa.shape),
        ] + [_const_spec(t.shape) for t in tabs],
        out_specs=[row_spec(w) for w in widths],
        out_shape=[jax.ShapeDtypeStruct((rows, w), dt) for w, dt in zip(widths, dtypes)],
        scratch_shapes=[pltpu.VMEM((D_MODEL, GLA_KW), F32)],
        compiler_params=pltpu.CompilerParams(
            dimension_semantics=("arbitrary",), vmem_limit_bytes=VMEM_LIMIT),
        name=f"proj_tm{tm}",
    )(h, nw, wgla, wga, wa2, ba, wswa, *tabs)


_FAR_KEY_SLABS = ((0, 32, 31), (0, 8, 7), (0, 16, 15), (0, 24, 23), (32, 40, 39), (32, 48, 47), (32, 56, 55))
_FAR_QUERY_SLABS = ((32, 64), (8, 32), (40, 64))
_FAR_QUERY_REFS = ((31, 32), (7, 8), (15, 8), (23, 8), (39, 8), (47, 8), (55, 8))
_FAR_Q = sum(b - a for a, b in _FAR_QUERY_SLABS)
_FAR_K = sum(b - a for a, b, _ in _FAR_KEY_SLABS)


def _far_mask_np():
    m = np.zeros((_FAR_Q, _FAR_K), np.float32)
    m[0:32, 0:32] = 1.0
    col = 32
    row = 32
    for _half in range(2):
        for grp in range(1, 4):
            m[row:row + 8, col:col + 8 * grp] = 1.0
            row += 8
            col += 8 * grp
    return m


def _near_select_np():
    s = np.zeros((GLA_SUB, GLA_HEADS, GLA_DK, GLA_HEADS, GLA_CHUNK // GLA_SUB, GLA_SUB), np.float32)
    for jl in range(GLA_SUB):
        for h in range(GLA_HEADS):
            s[jl, h, :, h, :, jl] = 1.0
    return s.reshape(GLA_SUB * GLA_KW, GLA_KW)


def _cumsum_chunks(x):
    rows = x.shape[0]
    x3 = x.reshape(rows // GLA_SUB, GLA_SUB, GLA_KW)
    sub = lax.broadcasted_iota(jnp.int32, x3.shape, 1)
    for s in (1, 2, 4):
        x3 = x3 + jnp.where(sub >= s, pltpu.roll(x3, s, axis=1), 0.0)
    x = x3.reshape(rows, GLA_KW)
    out = []
    for c in range(rows // GLA_CHUNK):
        carry = None
        for t in range(GLA_CHUNK // GLA_SUB):
            lo = c * GLA_CHUNK + t * GLA_SUB
            blk = x[lo:lo + GLA_SUB]
            if carry is not None:
                blk = blk + carry
            out.append(blk)
            carry = blk[GLA_SUB - 1:GLA_SUB, :]
    return jnp.concatenate(out, axis=0)


def _state_update(state_t, kc, vc_bf, bc, head_lane):
    b_last = bc[GLA_CHUNK - 1:GLA_CHUNK, :]
    k2 = (kc * jnp.exp2(b_last - bc)).astype(BF16)
    upd = None
    for h in range(GLA_HEADS):
        u = _dot_tn(vc_bf[:, h * GLA_DV:(h + 1) * GLA_DV], jnp.where(head_lane[h], k2, 0))
        upd = u if upd is None else upd + u
    return state_t * jnp.exp2(b_last) + upd


def _gla_chunk(qc, kc, vc, bc, near, state_t, head_lane, far_mask, kbd_mask, vbd_mask):
    vc_bf = vc.astype(BF16)

    def ref_rows(r, n):
        return jnp.broadcast_to(bc[r:r + 1, :], (n, GLA_KW))

    qb = (qc * jnp.exp2(bc)).astype(BF16)
    lhs = jnp.concatenate([jnp.where(head_lane[h], qb, 0) for h in range(GLA_HEADS)], axis=0)
    o_inter = _dot_nt(lhs, state_t.astype(BF16))

    q_rows = jnp.concatenate([qc[a:b] for a, b in _FAR_QUERY_SLABS], axis=0)
    bq = jnp.concatenate([bc[a:b] for a, b in _FAR_QUERY_SLABS], axis=0)
    rq = jnp.concatenate([ref_rows(r, n) for r, n in _FAR_QUERY_REFS], axis=0)
    q_far = (q_rows * jnp.exp2(bq - rq)).astype(BF16)
    k_rows = jnp.concatenate([kc[a:b] for a, b, _ in _FAR_KEY_SLABS], axis=0)
    bk = jnp.concatenate([bc[a:b] for a, b, _ in _FAR_KEY_SLABS], axis=0)
    rk = jnp.concatenate([ref_rows(r, b - a) for a, b, r in _FAR_KEY_SLABS], axis=0)
    k_far_t = (k_rows * jnp.exp2(rk - bk)).T.astype(BF16)
    k_bd = jnp.where(kbd_mask, jnp.concatenate([k_far_t] * GLA_HEADS, axis=1), 0)
    p = _dot(q_far, k_bd)
    p = jnp.where(far_mask, p, 0.0).astype(BF16)
    v_far = jnp.concatenate([vc[a:b] for a, b, _ in _FAR_KEY_SLABS], axis=0).astype(BF16)

    v_bd = jnp.where(vbd_mask, jnp.concatenate([vc_bf] * GLA_HEADS, axis=0), 0)
    o_near = _dot(near, v_bd)

    zeros8 = jnp.zeros((8, GLA_DV), F32)
    outs = []
    for h in range(GLA_HEADS):
        cols = slice(h * GLA_DV, (h + 1) * GLA_DV)
        of = _dot(p[:, cols], v_far[:, cols])
        top = jnp.concatenate([zeros8, of[32:56]], axis=0)
        bot = of[0:32] + jnp.concatenate([zeros8, of[56:80]], axis=0)
        outs.append(jnp.concatenate([top, bot], axis=0) + o_inter[h * GLA_CHUNK:(h + 1) * GLA_CHUNK])
    o = jnp.concatenate(outs, axis=1) + o_near
    return o, _state_update(state_t, kc, vc_bf, bc, head_lane)


def _gla_prepare(q_ref, k_ref, la_ref, sel_ref, b_ref, near_ref):
    q = q_ref[...].astype(F32)
    k = k_ref[...].astype(F32)
    b = _cumsum_chunks(la_ref[...] * LOG2E)
    nblk = GLA_ROWS // GLA_SUB
    q3 = q.reshape(nblk, GLA_SUB, GLA_KW)
    k3 = k.reshape(nblk, GLA_SUB, GLA_KW)
    b3 = b.reshape(nblk, GLA_SUB, GLA_KW)
    terms = []
    for jl in range(GLA_SUB):
        decay = jnp.exp2(jnp.minimum(b3 - b3[:, jl:jl + 1, :], 0.0))
        t = q3 * k3[:, jl:jl + 1, :] * decay
        terms.append(t.reshape(GLA_ROWS, GLA_KW).astype(BF16))
    near = _dot(jnp.concatenate(terms, axis=1), sel_ref[...])
    row = lax.broadcasted_iota(jnp.int32, (GLA_ROWS, GLA_KW), 0)
    col = lax.broadcasted_iota(jnp.int32, (GLA_ROWS, GLA_KW), 1)
    blocks = GLA_CHUNK // GLA_SUB
    near_mask = jnp.logical_and((row // GLA_SUB) % blocks == (col // GLA_SUB) % blocks,
                                col % GLA_SUB <= row % GLA_SUB)
    b_ref[...] = b
    near_ref[...] = jnp.where(near_mask, near, 0.0).astype(BF16)


def _gla_body(q_ref, k_ref, v_ref, la_ref, qn_ref, kn_ref, lan_ref, km_ref, vm_ref, lam_ref, sel_ref, fmask_ref,
              o_ref, state_ref, b_ref, near_ref):
    lane = lax.broadcasted_iota(jnp.int32, (1, GLA_KW), 1)
    head_lane = [(lane // GLA_DK) == h for h in range(GLA_HEADS)]

    @pl.when(pl.program_id(1) == 0)
    def _():
        bm = _cumsum_chunks(lam_ref[...] * LOG2E)
        zero = jnp.zeros((GLA_DV, GLA_KW), F32)
        state_ref[...] = _state_update(zero, km_ref[...].astype(F32), vm_ref[...], bm, head_lane)
        _gla_prepare(q_ref, k_ref, la_ref, sel_ref, b_ref, near_ref)

    q = q_ref[...].astype(F32)
    k = k_ref[...].astype(F32)
    v = v_ref[...].astype(F32)
    b = b_ref[...]
    near = near_ref[...]

    far_mask = fmask_ref[...] > 0.5
    krow = lax.broadcasted_iota(jnp.int32, (GLA_KW, GLA_HEADS * _FAR_K), 0)
    kcol = lax.broadcasted_iota(jnp.int32, (GLA_KW, GLA_HEADS * _FAR_K), 1)
    kbd_mask = (krow // GLA_DK) == (kcol // _FAR_K)
    vrow = lax.broadcasted_iota(jnp.int32, (GLA_HEADS * GLA_CHUNK, GLA_VW), 0)
    vcol = lax.broadcasted_iota(jnp.int32, (GLA_HEADS * GLA_CHUNK, GLA_VW), 1)
    vbd_mask = (vrow // GLA_CHUNK) == (vcol // GLA_DV)

    state_t = state_ref[...]
    for c in range(GLA_ROWS // GLA_CHUNK):
        rows = slice(c * GLA_CHUNK, (c + 1) * GLA_CHUNK)
        o, state_t = _gla_chunk(q[rows], k[rows], v[rows], b[rows], near[rows], state_t,
                                head_lane, far_mask, kbd_mask, vbd_mask)
        o_ref[rows, :] = o.astype(BF16)
    state_ref[...] = state_t
    _gla_prepare(qn_ref, kn_ref, lan_ref, sel_ref, b_ref, near_ref)


def _gla_call(gq, gk, gv, la, km, vm, lam, *, batch, seq_rows):
    nsteps = seq_rows // GLA_ROWS
    row_spec = lambda w: pl.BlockSpec((GLA_ROWS, w), lambda b, i: (b * nsteps + i, 0))
    next_spec = lambda w: pl.BlockSpec(
        (GLA_ROWS, w), lambda b, i: (b * nsteps + jnp.minimum(i + 1, nsteps - 1), 0))
    sel = jnp.asarray(_near_select_np(), BF16)
    fmask = jnp.asarray(np.tile(_far_mask_np(), (1, GLA_HEADS)), F32)
    return pl.pallas_call(
        _gla_body,
        grid=(batch, nsteps),
        in_specs=[
            row_spec(GLA_KW), row_spec(GLA_KW), row_spec(GLA_VW), row_spec(GLA_KW),
            next_spec(GLA_KW), next_spec(GLA_KW), next_spec(GLA_KW),
            _const_spec((GLA_CHUNK, GLA_KW)), _const_spec((GLA_CHUNK, GLA_VW)), _const_spec((GLA_CHUNK, GLA_KW)),
            _const_spec(sel.shape), _const_spec(fmask.shape),
        ],
        out_specs=row_spec(GLA_VW),
        out_shape=jax.ShapeDtypeStruct((batch * seq_rows, GLA_VW), BF16),
        scratch_shapes=[pltpu.VMEM((GLA_DV, GLA_KW), F32),
                        pltpu.VMEM((GLA_ROWS, GLA_KW), F32),
                        pltpu.VMEM((GLA_ROWS, GLA_KW), BF16)],
        compiler_params=pltpu.CompilerParams(
            dimension_semantics=("arbitrary", "arbitrary"), vmem_limit_bytes=VMEM_LIMIT),
        name="gla",
    )(gq, gk, gv, la, gq, gk, la, km, vm, lam, sel, fmask)


def _half_variants(x, low_lane):
    swapped = pltpu.roll(x, SWA_HEAD_DIM, axis=1)
    z = jnp.zeros_like(x)
    return (
        (jnp.where(low_lane, x, z).astype(BF16), jnp.where(low_lane, z, swapped).astype(BF16)),
        (jnp.where(low_lane, swapped, z).astype(BF16), jnp.where(low_lane, z, x).astype(BF16)),
    )


def _swa_body(sink_ref, q_ref, kp_ref, kc_ref, vp_ref, vc_ref, kmbd_ref, vmseg_ref, *rest):
    n_cast = len(rest) // 2
    o_ref = rest[n_cast]
    for w_ref, w_bf_ref in zip(rest[:n_cast], rest[n_cast + 1:]):
        w_bf_ref[...] = w_ref[...].astype(BF16)
    first = pl.program_id(1) == 0
    t = SWA_BLOCK
    lane = lax.broadcasted_iota(jnp.int32, (1, LANES), 1)
    low_lane = lane < SWA_HEAD_DIM
    meta_head = lane // N_META
    k_var = _half_variants(jnp.concatenate([kp_ref[...], kc_ref[...]], axis=0).astype(F32), low_lane)
    v_var = _half_variants(jnp.concatenate([vp_ref[...], vc_ref[...]], axis=0).astype(F32), low_lane)
    ones = jnp.ones((2 * t, LANES), BF16)

    r = lax.broadcasted_iota(jnp.int32, (t, t), 0)
    c = lax.broadcasted_iota(jnp.int32, (t, t), 1)
    causal = c <= r
    not_causal = jnp.logical_not(causal)
    not_causal_first = jnp.logical_and(not_causal, jnp.logical_not(first))

    for j in range(SWA_QBLOCKS):
        qj = q_ref[j * t:(j + 1) * t, :]
        keys = slice(j * t, (j + 2) * t)
        use_prev = not_causal_first if j == 0 else not_causal
        sm = _dot(qj, kmbd_ref[...])
        pv = [None] * SWA_Q_HEADS
        m_head = [None] * SWA_Q_HEADS
        for g in range(SWA_KV_HEADS):
            lhs = jnp.concatenate([qj[:, 2 * g * LANES:(2 * g + 1) * LANES],
                                   qj[:, (2 * g + 1) * LANES:(2 * g + 2) * LANES]], axis=0)
            for half in range(2):
                s_all = _dot_nt(lhs, k_var[g][half][keys])
                rhs = jnp.concatenate([v_var[g][half][keys], ones], axis=1)
                for grp in range(2):
                    head = 4 * g + 2 * grp + half
                    s = s_all[grp * t:(grp + 1) * t]
                    s_sel = jnp.where(causal, s[:, t:2 * t], jnp.where(use_prev, s[:, 0:t], NEG_INF))
                    sm_h = jnp.where(meta_head == head, sm, NEG_INF)
                    m = jnp.max(jnp.maximum(s_sel, sm_h), axis=-1, keepdims=True)
                    m = jnp.maximum(m, sink_ref[0, head])
                    p = jnp.exp(s_sel - m)
                    p_band = jnp.concatenate([jnp.where(causal, 0.0, p), jnp.where(causal, p, 0.0)], axis=1)
                    pv[head] = _dot(p_band.astype(BF16), rhs)
                    m_head[head] = m
        m_meta = jnp.zeros((t, LANES), F32)
        for head in range(SWA_Q_HEADS):
            m_meta = jnp.where(meta_head == head, m_head[head], m_meta)
        pm = jnp.exp(sm - m_meta).astype(BF16)
        mo = _dot(pm, vmseg_ref[...])
        for grp in range(SWA_Q_HEADS // 2):
            he, ho = 2 * grp, 2 * grp + 1
            cols = slice(grp * LANES, (grp + 1) * LANES)
            num = pv[he][:, 0:LANES] + pv[ho][:, 0:LANES] + mo[:, cols]
            sink_term = jnp.where(low_lane, jnp.exp(sink_ref[0, he] - m_head[he]),
                                  jnp.exp(sink_ref[0, ho] - m_head[ho]))
            den = (jnp.where(low_lane, pv[he][:, LANES:], pv[ho][:, LANES:])
                   + mo[:, SWA_QW + grp * LANES:SWA_QW + (grp + 1) * LANES] + sink_term)
            o_ref[j * t:(j + 1) * t, cols] = (num / den).astype(BF16)


def _swa_meta_operands(sk_m, sv_m):
    eye = jnp.eye(SWA_Q_HEADS, dtype=F32)
    per_head = lambda a: jnp.stack(
        [a[:, (h // 4) * SWA_HEAD_DIM:(h // 4 + 1) * SWA_HEAD_DIM] for h in range(SWA_Q_HEADS)]).astype(F32)
    km = per_head(sk_m)
    vm = per_head(sv_m)
    kmbd = jnp.einsum('hmd,hk->hdkm', km, eye).reshape(SWA_QW, SWA_Q_HEADS * N_META)
    vmbd = jnp.einsum('hmd,hk->hmkd', vm, eye).reshape(SWA_Q_HEADS * N_META, SWA_QW)
    seg = jnp.kron(eye, jnp.ones((N_META, SWA_HEAD_DIM), F32))
    return kmbd.astype(BF16), jnp.concatenate([vmbd, seg], axis=1).astype(BF16)


def _swa_call(sq, sk, sv, kmbd, vmseg, sinks, cast_weights, *, batch, seq_rows):
    nb = seq_rows // SWA_BLOCK
    step = SWA_QBLOCKS * SWA_BLOCK
    nsteps = seq_rows // step
    cur = lambda w: pl.BlockSpec((step, w), lambda b, i: (b * nsteps + i, 0))
    prev = lambda w: pl.BlockSpec(
        (SWA_BLOCK, w), lambda b, i: (b * nb + jnp.maximum(i * SWA_QBLOCKS - 1, 0), 0))
    slabs = math.gcd(batch * nsteps, SWA_CAST_SLABS)
    steps_per_slab = batch * nsteps // slabs
    slab = lambda w: pl.BlockSpec(
        (w.shape[0] // slabs, w.shape[1]), lambda b, i: ((b * nsteps + i) // steps_per_slab, 0))
    return pl.pallas_call(
        _swa_body,
        grid=(batch, nsteps),
        in_specs=[
            pl.BlockSpec(memory_space=pltpu.SMEM),
            cur(SWA_QW), prev(SWA_KW), cur(SWA_KW), prev(SWA_KW), cur(SWA_KW),
            _const_spec(kmbd.shape), _const_spec(vmseg.shape),
        ] + [slab(w) for w in cast_weights],
        out_specs=[cur(SWA_QW)] + [slab(w) for w in cast_weights],
        out_shape=[jax.ShapeDtypeStruct((batch * seq_rows, SWA_QW), BF16)]
        + [jax.ShapeDtypeStruct(w.shape, BF16) for w in cast_weights],
        compiler_params=pltpu.CompilerParams(
            dimension_semantics=("arbitrary", "arbitrary"), vmem_limit_bytes=VMEM_LIMIT),
        name="swa",
    )(sinks, sq, sk, sk, sv, sv, kmbd, vmseg, *cast_weights)


def _mix_ffn_body(og_ref, gg_ref, os_ref, h_ref, gnw_ref, snw_ref, wout_ref, pnw_ref,
                  pre_ref, wg_ref, wu_ref, wd_ref, post_ref, o_ref, act_ref):
    og = og_ref[...].astype(F32)
    gg = gg_ref[...].astype(F32)
    gnw = gnw_ref[...]
    parts = []
    for h in range(GLA_HEADS):
        cols = slice(h * GLA_DV, (h + 1) * GLA_DV)
        gate = gg[:, cols]
        parts.append(_rms(og[:, cols], gnw) * (gate * jax.nn.sigmoid(gate)))
    parts.append(_rms(os_ref[...].astype(F32), snw_ref[...]))
    mixed = jnp.concatenate(parts, axis=1).astype(wout_ref.dtype)
    m = _dot(mixed, wout_ref[...])
    h2 = h_ref[...] + _rms(m, pnw_ref[...])
    o_ref[...] = _ffn_rows(h2, pre_ref, wg_ref, wu_ref, wd_ref, post_ref, act_ref)


def _mix_ffn_call(og, gg, osw, h, gnw, snw, wout, pnw, pre, wg, wu, wd, post, *, tm):
    rows = h.shape[0]
    row_spec = lambda w: pl.BlockSpec((tm, w), lambda i: (i, 0))
    return pl.pallas_call(
        _mix_ffn_body,
        grid=(rows // tm,),
        in_specs=[
            row_spec(GLA_VW), row_spec(GLA_VW), row_spec(SWA_QW), row_spec(D_MODEL),
            _const_spec((1, GLA_DV)), _const_spec((1, SWA_QW)), _const_spec((D_MODEL, D_MODEL)),
            _const_spec((1, D_MODEL)),
            _const_spec((1, D_MODEL)),
            _const_spec((D_MODEL, D_FF)),
            _const_spec((D_MODEL, D_FF)),
            _const_spec((D_FF, D_MODEL)),
            _const_spec((1, D_MODEL)),
        ],
        out_specs=row_spec(D_MODEL),
        out_shape=jax.ShapeDtypeStruct((rows, D_MODEL), F32),
        scratch_shapes=[pltpu.VMEM((tm, D_FF), wg.dtype)],
        compiler_params=pltpu.CompilerParams(
            dimension_semantics=("arbitrary",), vmem_limit_bytes=VMEM_LIMIT),
        name="mix_ffn",
    )(og, gg, osw, h, gnw, snw, wout, pnw, pre, wg, wu, wd, post)


def kernel(x, meta_tokens, ffn1_pre_norm, ffn1_w_gate, ffn1_w_up, ffn1_w_down, ffn1_post_norm, mix_pre_norm, w_in, gla_w_a2, gla_b_a, gla_out_norm, swa_sinks, swa_out_norm, w_out, mix_post_norm, ffn2_pre_norm, ffn2_w_gate, ffn2_w_up, ffn2_w_down, ffn2_post_norm):
    batch, seq, _ = x.shape
    depth = ffn1_pre_norm.shape[0]
    assert depth == 1 and seq % GLA_ROWS == 0
    row = lambda a: a.reshape(1, -1).astype(F32)
    tm = 512

    w = w_in[0]
    n_gla = 2 * GLA_KW + 2 * GLA_VW
    wgla = w[:, :n_gla]
    wga = w[:, n_gla:n_gla + N_META]
    wswa = w[:, n_gla + N_META:]
    wa2 = gla_w_a2[0]
    ba = row(gla_b_a[0])

    ffn1 = (row(ffn1_pre_norm[0]), ffn1_w_gate[0], ffn1_w_up[0], ffn1_w_down[0], row(ffn1_post_norm[0]))
    proj_w =(row(mix_pre_norm[0]), wgla, wga, wa2, ba, wswa)

    hm = _ffn_call(meta_tokens.astype(F32), *ffn1, tm=N_META)
    _, gk_m, gv_m, _, la_m, _, sk_m, sv_m = _proj_call(
        hm, *proj_w, tm=N_META, sub=N_META, seq_rows=N_META, first_pos=0)
    front = GLA_CHUNK - N_META
    pad_front = lambda a: jnp.pad(a, ((front, 0), (0, 0)))

    xs = x.reshape(batch * seq, D_MODEL)
    h1 = _ffn_call(xs, *ffn1, tm=tm)
    gq, gk, gv, gg, la, sq, sk, sv = _proj_call(
        h1, *proj_w, tm=tm, sub=tm // 2, seq_rows=seq, first_pos=N_META)
    o_gla = _gla_call(gq, gk, gv, la, pad_front(gk_m), pad_front(gv_m), pad_front(la_m),
                      batch=batch, seq_rows=seq)
    kmbd, vmseg = _swa_meta_operands(sk_m, sv_m)
    o_swa, wout_bf, wg_bf, wu_bf, wd_bf = _swa_call(
        sq, sk, sv, kmbd, vmseg, swa_sinks.reshape(1, SWA_Q_HEADS).astype(F32),
        (w_out[0], ffn2_w_gate[0], ffn2_w_up[0], ffn2_w_down[0]), batch=batch, seq_rows=seq)
    out = _mix_ffn_call(o_gla, gg, o_swa, h1, row(gla_out_norm[0]), row(swa_out_norm[0]),
                        wout_bf, row(mix_post_norm[0]), row(ffn2_pre_norm[0]), wg_bf, wu_bf, wd_bf,
                        row(ffn2_post_norm[0]), tm=2 * tm)
    return out.reshape(batch, seq, D_MODEL)
```

```python
import functools
import math

import numpy as np
import jax
import jax.numpy as jnp
from jax import lax
from jax.experimental import pallas as pl
from jax.experimental.pallas import tpu as pltpu

F32 = jnp.float32
BF16 = jnp.bfloat16

D_MODEL = 1024
D_FF = 2816
N_META = 16
NORM_EPS = 1e-6
NEG_INF = -1e30
ROPE_THETA = 10000.0

GLA_HEADS = 4
GLA_DK = 64
GLA_DV = 128
GLA_KW = GLA_HEADS * GLA_DK
GLA_VW = GLA_HEADS * GLA_DV
GLA_TAU = 16.0
GLA_CHUNK = 64
GLA_SUB = 8
GLA_ROWS = 512
LOG2E = 1.4426950408889634

SWA_HEAD_DIM = 64
SWA_Q_HEADS = 8
SWA_KV_HEADS = 2
SWA_QW = SWA_Q_HEADS * SWA_HEAD_DIM
SWA_KW = SWA_KV_HEADS * SWA_HEAD_DIM
SWA_BLOCK = 128
SWA_QBLOCKS = 4
SWA_CAST_SLABS = 16

LANES = 128
FF_CHUNK = 256

VMEM_LIMIT = 56 * 1024 * 1024


def _dot(a, b):
    return jnp.dot(a, b, preferred_element_type=F32)


def _dot_nt(a, b):
    return lax.dot_general(a, b, (((1,), (1,)), ((), ())), preferred_element_type=F32)


def _dot_tn(a, b):
    return lax.dot_general(a, b, (((0,), (0,)), ((), ())), preferred_element_type=F32)


def _rms(x, w):
    ms = jnp.mean(x * x, axis=-1, keepdims=True)
    return x * lax.rsqrt(ms + NORM_EPS) * w


def _const_spec(shape):
    return pl.BlockSpec(shape, lambda *_: (0,) * len(shape), pipeline_mode=pl.Buffered(1))


def _ffn_rows(x, pre_ref, wg_ref, wu_ref, wd_ref, post_ref, act_ref):
    xn = _rms(x, pre_ref[...]).astype(wg_ref.dtype)
    for c in range(D_FF // FF_CHUNK):
        cols = slice(c * FF_CHUNK, (c + 1) * FF_CHUNK)
        g = _dot(xn, wg_ref[:, cols])
        u = _dot(xn, wu_ref[:, cols])
        act_ref[:, cols] = (g * jax.nn.sigmoid(g) * u).astype(act_ref.dtype)
    f = _dot(act_ref[...], wd_ref[...])
    return x + 0.5 * _rms(f, post_ref[...])


def _ffn_body(x_ref, pre_ref, wg_ref, wu_ref, wd_ref, post_ref, o_ref, act_ref):
    o_ref[...] = _ffn_rows(x_ref[...], pre_ref, wg_ref, wu_ref, wd_ref, post_ref, act_ref)


def _ffn_call(x, pre, wg, wu, wd, post, *, tm):
    rows = x.shape[0]
    return pl.pallas_call(
        _ffn_body,
        grid=(rows // tm,),
        in_specs=[
            pl.BlockSpec((tm, D_MODEL), lambda i: (i, 0)),
            _const_spec((1, D_MODEL)),
            _const_spec((D_MODEL, D_FF)),
            _const_spec((D_MODEL, D_FF)),
            _const_spec((D_FF, D_MODEL)),
            _const_spec((1, D_MODEL)),
        ],
        out_specs=pl.BlockSpec((tm, D_MODEL), lambda i: (i, 0)),
        out_shape=jax.ShapeDtypeStruct((rows, D_MODEL), F32),
        scratch_shapes=[pltpu.VMEM((tm, D_FF), wg.dtype)],
        compiler_params=pltpu.CompilerParams(
            dimension_semantics=("arbitrary",), vmem_limit_bytes=VMEM_LIMIT),
        name=f"ffn_tm{tm}",
    )(x, pre, wg, wu, wd, post)


def _rope(x, cos, sin_signed, first_half):
    out = []
    for g in range(x.shape[1] // LANES):
        xg = x[:, g * LANES:(g + 1) * LANES]
        fwd = pltpu.roll(xg, LANES - 32, axis=1)
        bwd = pltpu.roll(xg, 32, axis=1)
        out.append(xg * cos + jnp.where(first_half, fwd, bwd) * sin_signed)
    return out[0] if len(out) == 1 else jnp.concatenate(out, axis=1)


def _proj_body(h_ref, nw_ref, wt_ref, wa2_ref, ba_ref,
               row_cos_ref, row_sin_ref, step_cos_ref, step_sin_ref,
               gq_ref, gk_ref, gv_ref, gg_ref, la_ref, sq_ref, sk_ref, sv_ref,
               wgla_ref, wz_ref, wswa_ref, *, sub, seq_blocks):
    tm = h_ref.shape[0]
    n_gla = 2 * GLA_KW + 2 * GLA_VW

    @pl.when(pl.program_id(0) == 0)
    def _():
        wgla_ref[...] = wt_ref[0:n_gla, :].T
        wswa_ref[...] = wt_ref[n_gla + N_META:, :].T
        wz_ref[...] = lax.dot_general(wt_ref[n_gla:n_gla + N_META, :], wa2_ref[...], (((0,), (0,)), ((), ())),
                                      preferred_element_type=F32, precision=lax.Precision.HIGHEST)

    lane = lax.broadcasted_iota(jnp.int32, (1, LANES), 1)
    first_half = (lane % SWA_HEAD_DIM) < (SWA_HEAD_DIM // 2)
    step = pl.program_id(0) % seq_blocks
    ca = step_cos_ref[pl.ds(step, 1), :]
    sa = step_sin_ref[pl.ds(step, 1), :]
    for r0 in range(0, tm, sub):
        rows = slice(r0, r0 + sub)
        hn = _rms(h_ref[rows, :], nw_ref[...])
        pg = _dot(hn, wgla_ref[...])
        gq_ref[rows, :] = (pg[:, 0:GLA_KW] * (GLA_DK ** -0.5)).astype(BF16)
        gk_ref[rows, :] = pg[:, GLA_KW:2 * GLA_KW].astype(BF16)
        gv_ref[rows, :] = pg[:, 2 * GLA_KW:2 * GLA_KW + GLA_VW].astype(BF16)
        gg_ref[rows, :] = pg[:, 2 * GLA_KW + GLA_VW:].astype(BF16)

        z = _dot(hn, wz_ref[...]) + ba_ref[...]
        log_sig = jnp.minimum(z, 0.0) - jnp.log1p(jnp.exp(-jnp.abs(z)))
        la_ref[rows, :] = log_sig * (1.0 / GLA_TAU)

        ps = _dot(hn, wswa_ref[...])
        cb = row_cos_ref[rows, :]
        sb = row_sin_ref[rows, :]
        cos = ca * cb - sa * sb
        sin_signed = sa * cb + ca * sb
        sq = _rope(ps[:, 0:SWA_QW], cos, sin_signed, first_half)
        sq_ref[rows, :] = (sq * (SWA_HEAD_DIM ** -0.5)).astype(BF16)
        sk_ref[rows, :] = _rope(ps[:, SWA_QW:SWA_QW + SWA_KW], cos, sin_signed, first_half).astype(BF16)
        sv_ref[rows, :] = ps[:, SWA_QW + SWA_KW:].astype(BF16)


def _rope_tables(first_pos, tm, seq_blocks):
    inv_freq = 1.0 / (ROPE_THETA ** (np.arange(0, SWA_HEAD_DIM, 2, dtype=np.float64) / SWA_HEAD_DIM))

    def tables(pos):
        ang = pos.astype(np.float64)[:, None] * inv_freq[None, :]
        cos, sin = np.cos(ang), np.sin(ang)
        return (jnp.asarray(np.concatenate([cos, cos, cos, cos], axis=-1), F32),
                jnp.asarray(np.concatenate([-sin, sin, -sin, sin], axis=-1), F32))

    return tables(first_pos + np.arange(tm)) + tables(tm * np.arange(seq_blocks))


def _proj_call(h, nw, wt, wa2, ba, *, tm, sub, seq_rows, first_pos):
    rows = h.shape[0]
    n_gla = 2 * GLA_KW + 2 * GLA_VW
    seq_blocks = seq_rows // tm
    row_spec = lambda w: pl.BlockSpec((tm, w), lambda i: (i, 0))
    tabs = _rope_tables(first_pos, tm, seq_blocks)
    widths = (GLA_KW, GLA_KW, GLA_VW, GLA_VW, GLA_KW, SWA_QW, SWA_KW, SWA_KW)
    dtypes = (BF16, BF16, BF16, BF16, F32, BF16, BF16, BF16)
    return pl.pallas_call(
        functools.partial(_proj_body, sub=sub, seq_blocks=seq_blocks),
        grid=(rows // tm,),
        in_specs=[
            row_spec(D_MODEL),
            _const_spec((1, D_MODEL)),
            _const_spec(wt.shape),
            _const_spec(wa2.shape),
            _const_spec((1, GLA_KW)),
        ] + [_const_spec(t.shape) for t in tabs],
        out_specs=[row_spec(w) for w in widths],
        out_shape=[jax.ShapeDtypeStruct((rows, w), dt) for w, dt in zip(widths, dtypes)],
        scratch_shapes=[pltpu.VMEM((D_MODEL, n_gla), F32),
                        pltpu.VMEM((D_MODEL, GLA_KW), F32),
                        pltpu.VMEM((D_MODEL, SWA_QW + 2 * SWA_KW), F32)],
        compiler_params=pltpu.CompilerParams(
            dimension_semantics=("arbitrary",), vmem_limit_bytes=VMEM_LIMIT),
        name=f"proj_tm{tm}",
    )(h, nw, wt, wa2, ba, *tabs)


_FAR_KEY_SLABS = ((0, 32, 31), (0, 8, 7), (0, 16, 15), (0, 24, 23), (32, 40, 39), (32, 48, 47), (32, 56, 55))
_FAR_QUERY_SLABS = ((32, 64), (8, 32), (40, 64))
_FAR_QUERY_REFS = ((31, 32), (7, 8), (15, 8), (23, 8), (39, 8), (47, 8), (55, 8))
_FAR_Q = sum(b - a for a, b in _FAR_QUERY_SLABS)
_FAR_K = sum(b - a for a, b, _ in _FAR_KEY_SLABS)


def _far_mask_np():
    m = np.zeros((_FAR_Q, _FAR_K), np.float32)
    m[0:32, 0:32] = 1.0
    col = 32
    row = 32
    for _half in range(2):
        for grp in range(1, 4):
            m[row:row + 8, col:col + 8 * grp] = 1.0
            row += 8
            col += 8 * grp
    return m


def _near_select_np():
    s = np.zeros((GLA_SUB, GLA_HEADS, GLA_DK, GLA_HEADS, GLA_CHUNK // GLA_SUB, GLA_SUB), np.float32)
    for jl in range(GLA_SUB):
        for h in range(GLA_HEADS):
            s[jl, h, :, h, :, jl] = 1.0
    return s.reshape(GLA_SUB * GLA_KW, GLA_KW)


def _cumsum_chunks(x):
    rows = x.shape[0]
    x3 = x.reshape(rows // GLA_SUB, GLA_SUB, GLA_KW)
    sub = lax.broadcasted_iota(jnp.int32, x3.shape, 1)
    for s in (1, 2, 4):
        x3 = x3 + jnp.where(sub >= s, pltpu.roll(x3, s, axis=1), 0.0)
    x = x3.reshape(rows, GLA_KW)
    out = []
    for c in range(rows // GLA_CHUNK):
        carry = None
        for t in range(GLA_CHUNK // GLA_SUB):
            lo = c * GLA_CHUNK + t * GLA_SUB
            blk = x[lo:lo + GLA_SUB]
            if carry is not None:
                blk = blk + carry
            out.append(blk)
            carry = blk[GLA_SUB - 1:GLA_SUB, :]
    return jnp.concatenate(out, axis=0)


def _state_update(state_t, kc, vc_bf, bc, head_lane):
    b_last = bc[GLA_CHUNK - 1:GLA_CHUNK, :]
    k2 = (kc * jnp.exp2(b_last - bc)).astype(BF16)
    upd = None
    for h in range(GLA_HEADS):
        u = _dot_tn(vc_bf[:, h * GLA_DV:(h + 1) * GLA_DV], jnp.where(head_lane[h], k2, 0))
        upd = u if upd is None else upd + u
    return state_t * jnp.exp2(b_last) + upd


def _gla_chunk(qc, kc, vc, bc, near, state_t, head_lane, far_mask, kbd_mask, vbd_mask):
    vc_bf = vc.astype(BF16)

    def ref_rows(r, n):
        return jnp.broadcast_to(bc[r:r + 1, :], (n, GLA_KW))

    qb = (qc * jnp.exp2(bc)).astype(BF16)
    lhs = jnp.concatenate([jnp.where(head_lane[h], qb, 0) for h in range(GLA_HEADS)], axis=0)
    o_inter = _dot_nt(lhs, state_t.astype(BF16))

    q_rows = jnp.concatenate([qc[a:b] for a, b in _FAR_QUERY_SLABS], axis=0)
    bq = jnp.concatenate([bc[a:b] for a, b in _FAR_QUERY_SLABS], axis=0)
    rq = jnp.concatenate([ref_rows(r, n) for r, n in _FAR_QUERY_REFS], axis=0)
    q_far = (q_rows * jnp.exp2(bq - rq)).astype(BF16)
    k_rows = jnp.concatenate([kc[a:b] for a, b, _ in _FAR_KEY_SLABS], axis=0)
    bk = jnp.concatenate([bc[a:b] for a, b, _ in _FAR_KEY_SLABS], axis=0)
    rk = jnp.concatenate([ref_rows(r, b - a) for a, b, r in _FAR_KEY_SLABS], axis=0)
    k_far_t = (k_rows * jnp.exp2(rk - bk)).T.astype(BF16)
    k_bd = jnp.where(kbd_mask, jnp.concatenate([k_far_t] * GLA_HEADS, axis=1), 0)
    p = _dot(q_far, k_bd)
    p = jnp.where(far_mask, p, 0.0).astype(BF16)
    v_far = jnp.concatenate([vc[a:b] for a, b, _ in _FAR_KEY_SLABS], axis=0).astype(BF16)

    v_bd = jnp.where(vbd_mask, jnp.concatenate([vc_bf] * GLA_HEADS, axis=0), 0)
    o_near = _dot(near, v_bd)

    zeros8 = jnp.zeros((8, GLA_DV), F32)
    outs = []
    for h in range(GLA_HEADS):
        cols = slice(h * GLA_DV, (h + 1) * GLA_DV)
        of = _dot(p[:, cols], v_far[:, cols])
        top = jnp.concatenate([zeros8, of[32:56]], axis=0)
        bot = of[0:32] + jnp.concatenate([zeros8, of[56:80]], axis=0)
        outs.append(jnp.concatenate([top, bot], axis=0) + o_inter[h * GLA_CHUNK:(h + 1) * GLA_CHUNK])
    o = jnp.concatenate(outs, axis=1) + o_near
    return o, _state_update(state_t, kc, vc_bf, bc, head_lane)


def _gla_prepare(q_ref, k_ref, la_ref, sel_ref, b_ref, near_ref):
    q = q_ref[...].astype(F32)
    k = k_ref[...].astype(F32)
    b = _cumsum_chunks(la_ref[...] * LOG2E)
    nblk = GLA_ROWS // GLA_SUB
    q3 = q.reshape(nblk, GLA_SUB, GLA_KW)
    k3 = k.reshape(nblk, GLA_SUB, GLA_KW)
    b3 = b.reshape(nblk, GLA_SUB, GLA_KW)
    terms = []
    for jl in range(GLA_SUB):
        decay = jnp.exp2(jnp.minimum(b3 - b3[:, jl:jl + 1, :], 0.0))
        t = q3 * k3[:, jl:jl + 1, :] * decay
        terms.append(t.reshape(GLA_ROWS, GLA_KW).astype(BF16))
    near = _dot(jnp.concatenate(terms, axis=1), sel_ref[...])
    row = lax.broadcasted_iota(jnp.int32, (GLA_ROWS, GLA_KW), 0)
    col = lax.broadcasted_iota(jnp.int32, (GLA_ROWS, GLA_KW), 1)
    blocks = GLA_CHUNK // GLA_SUB
    near_mask = jnp.logical_and((row // GLA_SUB) % blocks == (col // GLA_SUB) % blocks,
                                col % GLA_SUB <= row % GLA_SUB)
    b_ref[...] = b
    near_ref[...] = jnp.where(near_mask, near, 0.0).astype(BF16)


def _gla_body(q_ref, k_ref, v_ref, la_ref, qn_ref, kn_ref, lan_ref, km_ref, vm_ref, lam_ref, sel_ref, fmask_ref,
              o_ref, state_ref, b_ref, near_ref):
    lane = lax.broadcasted_iota(jnp.int32, (1, GLA_KW), 1)
    head_lane = [(lane // GLA_DK) == h for h in range(GLA_HEADS)]

    @pl.when(pl.program_id(1) == 0)
    def _():
        bm = _cumsum_chunks(lam_ref[...] * LOG2E)
        zero = jnp.zeros((GLA_DV, GLA_KW), F32)
        state_ref[...] = _state_update(zero, km_ref[...].astype(F32), vm_ref[...], bm, head_lane)
        _gla_prepare(q_ref, k_ref, la_ref, sel_ref, b_ref, near_ref)

    q = q_ref[...].astype(F32)
    k = k_ref[...].astype(F32)
    v = v_ref[...].astype(F32)
    b = b_ref[...]
    near = near_ref[...]

    far_mask = fmask_ref[...] > 0.5
    krow = lax.broadcasted_iota(jnp.int32, (GLA_KW, GLA_HEADS * _FAR_K), 0)
    kcol = lax.broadcasted_iota(jnp.int32, (GLA_KW, GLA_HEADS * _FAR_K), 1)
    kbd_mask = (krow // GLA_DK) == (kcol // _FAR_K)
    vrow = lax.broadcasted_iota(jnp.int32, (GLA_HEADS * GLA_CHUNK, GLA_VW), 0)
    vcol = lax.broadcasted_iota(jnp.int32, (GLA_HEADS * GLA_CHUNK, GLA_VW), 1)
    vbd_mask = (vrow // GLA_CHUNK) == (vcol // GLA_DV)

    state_t = state_ref[...]
    for c in range(GLA_ROWS // GLA_CHUNK):
        rows = slice(c * GLA_CHUNK, (c + 1) * GLA_CHUNK)
        o, state_t = _gla_chunk(q[rows], k[rows], v[rows], b[rows], near[rows], state_t,
                                head_lane, far_mask, kbd_mask, vbd_mask)
        o_ref[rows, :] = o.astype(BF16)
    state_ref[...] = state_t
    _gla_prepare(qn_ref, kn_ref, lan_ref, sel_ref, b_ref, near_ref)


def _gla_call(gq, gk, gv, la, km, vm, lam, *, batch, seq_rows):
    nsteps = seq_rows // GLA_ROWS
    row_spec = lambda w: pl.BlockSpec((GLA_ROWS, w), lambda b, i: (b * nsteps + i, 0))
    next_spec = lambda w: pl.BlockSpec(
        (GLA_ROWS, w), lambda b, i: (b * nsteps + jnp.minimum(i + 1, nsteps - 1), 0))
    sel = jnp.asarray(_near_select_np(), BF16)
    fmask = jnp.asarray(np.tile(_far_mask_np(), (1, GLA_HEADS)), F32)
    return pl.pallas_call(
        _gla_body,
        grid=(batch, nsteps),
        in_specs=[
            row_spec(GLA_KW), row_spec(GLA_KW), row_spec(GLA_VW), row_spec(GLA_KW),
            next_spec(GLA_KW), next_spec(GLA_KW), next_spec(GLA_KW),
            _const_spec((GLA_CHUNK, GLA_KW)), _const_spec((GLA_CHUNK, GLA_VW)), _const_spec((GLA_CHUNK, GLA_KW)),
            _const_spec(sel.shape), _const_spec(fmask.shape),
        ],
        out_specs=row_spec(GLA_VW),
        out_shape=jax.ShapeDtypeStruct((batch * seq_rows, GLA_VW), BF16),
        scratch_shapes=[pltpu.VMEM((GLA_DV, GLA_KW), F32),
                        pltpu.VMEM((GLA_ROWS, GLA_KW), F32),
                        pltpu.VMEM((GLA_ROWS, GLA_KW), BF16)],
        compiler_params=pltpu.CompilerParams(
            dimension_semantics=("arbitrary", "arbitrary"), vmem_limit_bytes=VMEM_LIMIT),
        name="gla",
    )(gq, gk, gv, la, gq, gk, la, km, vm, lam, sel, fmask)


def _half_variants(x, low_lane):
    swapped = pltpu.roll(x, SWA_HEAD_DIM, axis=1)
    z = jnp.zeros_like(x)
    return (
        (jnp.where(low_lane, x, z).astype(BF16), jnp.where(low_lane, z, swapped).astype(BF16)),
        (jnp.where(low_lane, swapped, z).astype(BF16), jnp.where(low_lane, z, x).astype(BF16)),
    )


def _swa_body(sink_ref, q_ref, kp_ref, kc_ref, vp_ref, vc_ref, kmbd_ref, vmseg_ref, *rest):
    n_cast = len(rest) // 2
    o_ref = rest[n_cast]
    for w_ref, w_bf_ref in zip(rest[:n_cast], rest[n_cast + 1:]):
        w_bf_ref[...] = w_ref[...].astype(BF16)
    first = pl.program_id(1) == 0
    t = SWA_BLOCK
    lane = lax.broadcasted_iota(jnp.int32, (1, LANES), 1)
    low_lane = lane < SWA_HEAD_DIM
    meta_head = lane // N_META
    k_var = _half_variants(jnp.concatenate([kp_ref[...], kc_ref[...]], axis=0).astype(F32), low_lane)
    v_var = _half_variants(jnp.concatenate([vp_ref[...], vc_ref[...]], axis=0).astype(F32), low_lane)
    ones = jnp.ones((2 * t, LANES), BF16)

    r = lax.broadcasted_iota(jnp.int32, (t, t), 0)
    c = lax.broadcasted_iota(jnp.int32, (t, t), 1)
    causal = c <= r
    not_causal = jnp.logical_not(causal)
    not_causal_first = jnp.logical_and(not_causal, jnp.logical_not(first))

    for j in range(SWA_QBLOCKS):
        qj = q_ref[j * t:(j + 1) * t, :]
        keys = slice(j * t, (j + 2) * t)
        use_prev = not_causal_first if j == 0 else not_causal
        sm = _dot(qj, kmbd_ref[...])
        pv = [None] * SWA_Q_HEADS
        m_head = [None] * SWA_Q_HEADS
        for g in range(SWA_KV_HEADS):
            lhs = jnp.concatenate([qj[:, 2 * g * LANES:(2 * g + 1) * LANES],
                                   qj[:, (2 * g + 1) * LANES:(2 * g + 2) * LANES]], axis=0)
            for half in range(2):
                s_all = _dot_nt(lhs, k_var[g][half][keys])
                rhs = jnp.concatenate([v_var[g][half][keys], ones], axis=1)
                for grp in range(2):
                    head = 4 * g + 2 * grp + half
                    s = s_all[grp * t:(grp + 1) * t]
                    s_sel = jnp.where(causal, s[:, t:2 * t], jnp.where(use_prev, s[:, 0:t], NEG_INF))
                    sm_h = jnp.where(meta_head == head, sm, NEG_INF)
                    m = jnp.max(jnp.maximum(s_sel, sm_h), axis=-1, keepdims=True)
                    m = jnp.maximum(m, sink_ref[0, head])
                    p = jnp.exp(s_sel - m)
                    p_band = jnp.concatenate([jnp.where(causal, 0.0, p), jnp.where(causal, p, 0.0)], axis=1)
                    pv[head] = _dot(p_band.astype(BF16), rhs)
                    m_head[head] = m
        m_meta = jnp.zeros((t, LANES), F32)
        for head in range(SWA_Q_HEADS):
            m_meta = jnp.where(meta_head == head, m_head[head], m_meta)
        pm = jnp.exp(sm - m_meta).astype(BF16)
        mo = _dot(pm, vmseg_ref[...])
        for grp in range(SWA_Q_HEADS // 2):
            he, ho = 2 * grp, 2 * grp + 1
            cols = slice(grp * LANES, (grp + 1) * LANES)
            num = pv[he][:, 0:LANES] + pv[ho][:, 0:LANES] + mo[:, cols]
            sink_term = jnp.where(low_lane, jnp.exp(sink_ref[0, he] - m_head[he]),
                                  jnp.exp(sink_ref[0, ho] - m_head[ho]))
            den = (jnp.where(low_lane, pv[he][:, LANES:], pv[ho][:, LANES:])
                   + mo[:, SWA_QW + grp * LANES:SWA_QW + (grp + 1) * LANES] + sink_term)
            o_ref[j * t:(j + 1) * t, cols] = (num / den).astype(BF16)


def _swa_meta_operands(sk_m, sv_m):
    eye = jnp.eye(SWA_Q_HEADS, dtype=F32)
    per_head = lambda a: jnp.stack(
        [a[:, (h // 4) * SWA_HEAD_DIM:(h // 4 + 1) * SWA_HEAD_DIM] for h in range(SWA_Q_HEADS)]).astype(F32)
    km = per_head(sk_m)
    vm = per_head(sv_m)
    kmbd = jnp.einsum('hmd,hk->hdkm', km, eye).reshape(SWA_QW, SWA_Q_HEADS * N_META)
    vmbd = jnp.einsum('hmd,hk->hmkd', vm, eye).reshape(SWA_Q_HEADS * N_META, SWA_QW)
    seg = jnp.kron(eye, jnp.ones((N_META, SWA_HEAD_DIM), F32))
    return kmbd.astype(BF16), jnp.concatenate([vmbd, seg], axis=1).astype(BF16)


def _swa_call(sq, sk, sv, kmbd, vmseg, sinks, cast_weights, *, batch, seq_rows):
    nb = seq_rows // SWA_BLOCK
    step = SWA_QBLOCKS * SWA_BLOCK
    nsteps = seq_rows // step
    cur = lambda w: pl.BlockSpec((step, w), lambda b, i: (b * nsteps + i, 0))
    prev = lambda w: pl.BlockSpec(
        (SWA_BLOCK, w), lambda b, i: (b * nb + jnp.maximum(i * SWA_QBLOCKS - 1, 0), 0))
    slabs = math.gcd(batch * nsteps, SWA_CAST_SLABS)
    steps_per_slab = batch * nsteps // slabs
    slab = lambda w: pl.BlockSpec(
        (w.shape[0] // slabs, w.shape[1]), lambda b, i: ((b * nsteps + i) // steps_per_slab, 0))
    return pl.pallas_call(
        _swa_body,
        grid=(batch, nsteps),
        in_specs=[
            pl.BlockSpec(memory_space=pltpu.SMEM),
            cur(SWA_QW), prev(SWA_KW), cur(SWA_KW), prev(SWA_KW), cur(SWA_KW),
            _const_spec(kmbd.shape), _const_spec(vmseg.shape),
        ] + [slab(w) for w in cast_weights],
        out_specs=[cur(SWA_QW)] + [slab(w) for w in cast_weights],
        out_shape=[jax.ShapeDtypeStruct((batch * seq_rows, SWA_QW), BF16)]
        + [jax.ShapeDtypeStruct(w.shape, BF16) for w in cast_weights],
        compiler_params=pltpu.CompilerParams(
            dimension_semantics=("arbitrary", "arbitrary"), vmem_limit_bytes=VMEM_LIMIT),
        name="swa",
    )(sinks, sq, sk, sk, sv, sv, kmbd, vmseg, *cast_weights)


def _mix_ffn_body(og_ref, gg_ref, os_ref, h_ref, gnw_ref, snw_ref, wout_ref, pnw_ref,
                  pre_ref, wg_ref, wu_ref, wd_ref, post_ref, o_ref, act_ref):
    og = og_ref[...].astype(F32)
    gg = gg_ref[...].astype(F32)
    gnw = gnw_ref[...]
    parts = []
    for h in range(GLA_HEADS):
        cols = slice(h * GLA_DV, (h + 1) * GLA_DV)
        gate = gg[:, cols]
        parts.append(_rms(og[:, cols], gnw) * (gate * jax.nn.sigmoid(gate)))
    parts.append(_rms(os_ref[...].astype(F32), snw_ref[...]))
    mixed = jnp.concatenate(parts, axis=1).astype(wout_ref.dtype)
    m = _dot(mixed, wout_ref[...])
    h2 = h_ref[...] + _rms(m, pnw_ref[...])
    o_ref[...] = _ffn_rows(h2, pre_ref, wg_ref, wu_ref, wd_ref, post_ref, act_ref)


def _mix_ffn_call(og, gg, osw, h, gnw, snw, wout, pnw, pre, wg, wu, wd, post, *, tm):
    rows = h.shape[0]
    row_spec = lambda w: pl.BlockSpec((tm, w), lambda i: (i, 0))
    return pl.pallas_call(
        _mix_ffn_body,
        grid=(rows // tm,),
        in_specs=[
            row_spec(GLA_VW), row_spec(GLA_VW), row_spec(SWA_QW), row_spec(D_MODEL),
            _const_spec((1, GLA_DV)), _const_spec((1, SWA_QW)), _const_spec((D_MODEL, D_MODEL)),
            _const_spec((1, D_MODEL)),
            _const_spec((1, D_MODEL)),
            _const_spec((D_MODEL, D_FF)),
            _const_spec((D_MODEL, D_FF)),
            _const_spec((D_FF, D_MODEL)),
            _const_spec((1, D_MODEL)),
        ],
        out_specs=row_spec(D_MODEL),
        out_shape=jax.ShapeDtypeStruct((rows, D_MODEL), F32),
        scratch_shapes=[pltpu.VMEM((tm, D_FF), wg.dtype)],
        compiler_params=pltpu.CompilerParams(
            dimension_semantics=("arbitrary",), vmem_limit_bytes=VMEM_LIMIT),
        name="mix_ffn",
    )(og, gg, osw, h, gnw, snw, wout, pnw, pre, wg, wu, wd, post)


def kernel(x, meta_tokens, ffn1_pre_norm, ffn1_w_gate, ffn1_w_up, ffn1_w_down, ffn1_post_norm, mix_pre_norm, w_in, gla_w_a2, gla_b_a, gla_out_norm, swa_sinks, swa_out_norm, w_out, mix_post_norm, ffn2_pre_norm, ffn2_w_gate, ffn2_w_up, ffn2_w_down, ffn2_post_norm):
    batch, seq, _ = x.shape
    depth = ffn1_pre_norm.shape[0]
    assert depth == 1 and seq % GLA_ROWS == 0
    row = lambda a: a.reshape(1, -1).astype(F32)
    tm = 512

    ba = row(gla_b_a[0])

    ffn1 = (row(ffn1_pre_norm[0]), ffn1_w_gate[0], ffn1_w_up[0], ffn1_w_down[0], row(ffn1_post_norm[0]))
    proj_w = (row(mix_pre_norm[0]), w_in[0].T, gla_w_a2[0], ba)

    hm = _ffn_call(meta_tokens.astype(F32), *ffn1, tm=N_META)
    _, gk_m, gv_m, _, la_m, _, sk_m, sv_m = _proj_call(
        hm, *proj_w, tm=N_META, sub=N_META, seq_rows=N_META, first_pos=0)
    front = GLA_CHUNK - N_META
    pad_front = lambda a: jnp.pad(a, ((front, 0), (0, 0)))

    xs = x.reshape(batch * seq, D_MODEL)
    h1 = _ffn_call(xs, *ffn1, tm=tm)
    gq, gk, gv, gg, la, sq, sk, sv = _proj_call(
        h1, *proj_w, tm=tm, sub=tm // 2, seq_rows=seq, first_pos=N_META)
    o_gla = _gla_call(gq, gk, gv, la, pad_front(gk_m), pad_front(gv_m), pad_front(la_m),
                      batch=batch, seq_rows=seq)
    kmbd, vmseg = _swa_meta_operands(sk_m, sv_m)
    o_swa, wout_bf, wg_bf, wu_bf, wd_bf = _swa_call(
        sq, sk, sv, kmbd, vmseg, swa_sinks.reshape(1, SWA_Q_HEADS).astype(F32),
        (w_out[0], ffn2_w_gate[0], ffn2_w_up[0], ffn2_w_down[0]), batch=batch, seq_rows=seq)
    out = _mix_ffn_call(o_gla, gg, o_swa, h1, row(gla_out_norm[0]), row(swa_out_norm[0]),
                        wout_bf, row(mix_post_norm[0]), row(ffn2_pre_norm[0]), wg_bf, wu_bf, wd_bf,
                        row(ffn2_post_norm[0]), tm=2 * tm)
    return out.reshape(batch, seq, D_MODEL)
```
